```python
import jax, jax.numpy as jnp
from jax import lax
import numpy as np

D_MODEL = 1024
BATCH = 2
SEQ = 8192
DEPTH = 1
DEC_BATCH = 4
DEC_SEQ = 8192
PAST_LEN = 128

N_META = 16
D_A = D_MODEL
H_A = 8
BW_A = D_A // H_A
CONV_W = 4
CONV_PAD = (2, 1)
RG_C = 8.0
D_B = D_MODEL
HGRN_EXPAND = 128
H_B = D_B // HGRN_EXPAND
DK_B = D_B // H_B
DV_B = D_B // H_B
CHUNK = 64
D_FF = 4 * D_MODEL
EPS = 1e-6
N_IN = 2 * D_A + 5 * D_B + 2 * D_MODEL

kernel_name = "hybrid_rglru_hgrn2_encoder"


def rmsnorm(x, g):
    xf = x.astype(jnp.float32)
    y = xf * lax.rsqrt(jnp.mean(xf * xf, axis=-1, keepdims=True) + EPS)
    return (y * g.astype(jnp.float32)).astype(x.dtype)


def centred_depthwise_conv(x, w, b):
    y = lax.conv_general_dilated(
        x, w[:, None, :].astype(x.dtype), window_strides=(1,), padding=[CONV_PAD],
        dimension_numbers=("NWC", "WIO", "NWC"), feature_group_count=x.shape[-1])
    return y + b.astype(x.dtype)


def _linear_combine(left, right):
    a1, b1 = left
    a2, b2 = right
    return a1 * a2, a2 * b1 + b2


def rglru_direction(xc, wa, ba, wx, bx, lam, reverse):
    B, L, _ = xc.shape
    xh = xc.reshape(B, L, H_A, BW_A)
    r = jax.nn.sigmoid(jnp.einsum("blhi,hij->blhj", xh, wa.astype(jnp.float32)).reshape(B, L, D_A)
                       + ba.astype(jnp.float32))
    i = jax.nn.sigmoid(jnp.einsum("blhi,hij->blhj", xh, wx.astype(jnp.float32)).reshape(B, L, D_A)
                       + bx.astype(jnp.float32))
    log_a = -RG_C * jax.nn.softplus(-lam.astype(jnp.float32)) * r
    a = jnp.exp(log_a)
    mult = jnp.sqrt(-jnp.expm1(2.0 * log_a))
    u = mult * (i * xc)
    _, h = lax.associative_scan(_linear_combine, (a, u), axis=1, reverse=reverse)
    return h


def gla_chunk_scan(q, k, v, logf, s0, chunk):
    B, T, H, K = q.shape
    V = v.shape[-1]
    n = T // chunk

    def to_chunks(t):
        return jnp.moveaxis(t.reshape(B, n, chunk, H, t.shape[-1]), 1, 0)

    qc, kc, vc, gc = (to_chunks(t) for t in (q, k, v, logf))
    bc = jnp.cumsum(gc, axis=2)
    causal = jnp.tril(jnp.ones((chunk, chunk), dtype=bool))[None, :, :, None, None]

    def step(S, inp):
        qi, ki, vi, bi = inp
        o_inter = jnp.einsum("bthk,bhkv->bthv", qi * jnp.exp(bi), S)
        diff = bi[:, :, None] - bi[:, None, :]
        decay = jnp.exp(jnp.where(causal, diff, -jnp.inf))
        scores = jnp.einsum("bthk,bshk,btshk->bhts", qi, ki, decay)
        o_intra = jnp.einsum("bhts,bshv->bthv", scores, vi)
        b_last = bi[:, -1]
        S_new = jnp.exp(b_last)[..., None] * S + jnp.einsum(
            "bshk,bshv->bhkv", ki * jnp.exp(b_last[:, None] - bi), vi)
        return S_new, o_inter + o_intra

    S_T, o = lax.scan(step, s0, (qc, kc, vc, bc))
    o = jnp.moveaxis(o, 0, 1).reshape(B, T, H, V)
    return o, S_T


def hgrn2_bidirectional(q, k_f, k_b, v, g_f, g_b):
    B = q.shape[0]
    m = N_META
    zero = jnp.zeros((B, H_B, DK_B, DV_B), jnp.float32)
    flip = lambda t: jnp.flip(t, axis=1)
    o_meta_f, s_meta = gla_chunk_scan(q[:, :m], k_f[:, :m], v[:, :m], g_f[:, :m], zero, m)
    o_real_f, _ = gla_chunk_scan(q[:, m:], k_f[:, m:], v[:, m:], g_f[:, m:], s_meta, CHUNK)
    o_real_b, s_real = gla_chunk_scan(flip(q[:, m:]), flip(k_b[:, m:]), flip(v[:, m:]),
                                      flip(g_b[:, m:]), zero, CHUNK)
    o_meta_b, _ = gla_chunk_scan(flip(q[:, :m]), flip(k_b[:, :m]), flip(v[:, :m]),
                                 flip(g_b[:, :m]), s_real, m)
    o_f = jnp.concatenate([o_meta_f, o_real_f], axis=1)
    o_b = jnp.concatenate([flip(o_meta_b), flip(o_real_b)], axis=1)
    return o_f + o_b


def hybrid_layer(x, lb, norm_mix_g, w_in, conv_w, conv_b, rg_wa, rg_ba, rg_wx, rg_bx, rg_lambda,
                 hg_norm_g, w_branch_a, w_branch_b, w_out, norm_mlp_g, w_mlp1, w_mlp2):
    B, L, _ = x.shape
    f32 = jnp.float32
    h = rmsnorm(x, norm_mix_g)
    z = h @ w_in.astype(h.dtype)
    sizes = [D_A, D_A, D_B, D_B, D_B, D_B, D_B, D_MODEL, D_MODEL]
    idx = np.cumsum(sizes)[:-1].tolist()
    xa, ya, q_raw, ff_raw, fb_raw, v_raw, og_raw, gate_a, gate_b = jnp.split(z, idx, axis=-1)

    xc = centred_depthwise_conv(xa, conv_w, conv_b).astype(f32)
    h_rnn = (rglru_direction(xc, rg_wa[0], rg_ba[0], rg_wx[0], rg_bx[0], rg_lambda[0], False)
             + rglru_direction(xc, rg_wa[1], rg_ba[1], rg_wx[1], rg_bx[1], rg_lambda[1], True))
    branch_a = (h_rnn * jax.nn.gelu(ya.astype(f32))).astype(x.dtype)

    heads = lambda t: t.astype(f32).reshape(B, L, H_B, -1)
    qh = heads(jax.nn.silu(q_raw.astype(f32)))
    vh = heads(v_raw)

    def forget(f_raw, lbd):
        f = lbd + (1.0 - lbd) * jax.nn.sigmoid(f_raw.astype(f32))
        return heads(1.0 - f), heads(jnp.log(f))

    k_f, g_f = forget(ff_raw, lb[0])
    k_b, g_b = forget(fb_raw, lb[1])
    o = hgrn2_bidirectional(qh, k_f, k_b, vh, g_f, g_b)
    o = o * lax.rsqrt(jnp.mean(o * o, axis=-1, keepdims=True) + EPS)
    o = o.reshape(B, L, D_B) * hg_norm_g.astype(f32) * jax.nn.silu(og_raw.astype(f32))
    branch_b = o.astype(x.dtype)

    merged = (jax.nn.sigmoid(gate_a) * (branch_a @ w_branch_a.astype(x.dtype))
              + jax.nn.sigmoid(gate_b) * (branch_b @ w_branch_b.astype(x.dtype)))
    x = x + merged @ w_out.astype(x.dtype)

    hm = rmsnorm(x, norm_mlp_g) @ w_mlp1.astype(x.dtype)
    x = x + jnp.square(jax.nn.relu(hm)) @ w_mlp2.astype(x.dtype)
    return x


def encode(x, meta_tokens, hg_lb_logits, norm_mix_g, w_in, conv_w, conv_b, rg_wa, rg_ba, rg_wx, rg_bx,
           rg_lambda, hg_norm_g, w_branch_a, w_branch_b, w_out, norm_mlp_g, w_mlp1, w_mlp2, final_norm_g):
    B = x.shape[0]
    meta = jnp.broadcast_to(meta_tokens.astype(x.dtype)[None], (B, N_META, D_MODEL))
    h = jnp.concatenate([meta, x], axis=1)
    lb_all = jnp.cumsum(jax.nn.softmax(hg_lb_logits.astype(jnp.float32), axis=0), axis=0)
    for l in range(DEPTH):
        h = hybrid_layer(h, lb_all[l], norm_mix_g[l], w_in[l], conv_w[l], conv_b[l], rg_wa[l], rg_ba[l],
                         rg_wx[l], rg_bx[l], rg_lambda[l], hg_norm_g[l], w_branch_a[l], w_branch_b[l],
                         w_out[l], norm_mlp_g[l], w_mlp1[l], w_mlp2[l])
    return rmsnorm(h[:, N_META:], final_norm_g)


def setup_inputs(seed: int = 0) -> dict:
    key = jax.random.key(seed)
    ks = jax.random.split(key, 24)
    nrm = lambda k, shape, s: jax.random.normal(k, shape, jnp.float32) * s
    u = jax.random.uniform(ks[12], (DEPTH, 2, D_A), jnp.float32, minval=0.9, maxval=0.999)
    root = u ** (1.0 / RG_C)
    rg_lambda = jnp.log(root) - jnp.log1p(-root)
    return {
        "x_prompt": nrm(ks[0], (BATCH, SEQ, D_MODEL), 1.0),
        "x_sample": nrm(ks[1], (DEC_BATCH, DEC_SEQ, D_MODEL), 1.0),
        "meta_tokens": nrm(ks[2], (N_META, D_MODEL), 1.0),
        "hg_lb_logits": nrm(ks[3], (DEPTH + 1, 2, D_B), 0.5),
        "norm_mix_g": 1.0 + nrm(ks[4], (DEPTH, D_MODEL), 0.02),
        "w_in": nrm(ks[5], (DEPTH, D_MODEL, N_IN), D_MODEL ** -0.5),
        "conv_w": nrm(ks[6], (DEPTH, CONV_W, D_A), CONV_W ** -0.5),
        "conv_b": nrm(ks[7], (DEPTH, D_A), 0.01),
        "rg_wa": nrm(ks[8], (DEPTH, 2, H_A, BW_A, BW_A), BW_A ** -0.5),
        "rg_ba": nrm(ks[9], (DEPTH, 2, D_A), 0.01),
        "rg_wx": nrm(ks[10], (DEPTH, 2, H_A, BW_A, BW_A), BW_A ** -0.5),
        "rg_bx": nrm(ks[11], (DEPTH, 2, D_A), 0.01),
        "rg_lambda": rg_lambda,
        "hg_norm_g": 1.0 + nrm(ks[13], (DEPTH, D_B), 0.02),
        "w_branch_a": nrm(ks[14], (DEPTH, D_A, D_MODEL), D_A ** -0.5),
        "w_branch_b": nrm(ks[15], (DEPTH, D_B, D_MODEL), D_B ** -0.5),
        "w_out": nrm(ks[16], (DEPTH, D_MODEL, D_MODEL), D_MODEL ** -0.5),
        "norm_mlp_g": 1.0 + nrm(ks[17], (DEPTH, D_MODEL), 0.02),
        "w_mlp1": nrm(ks[18], (DEPTH, D_MODEL, D_FF), D_MODEL ** -0.5),
        "w_mlp2": nrm(ks[19], (DEPTH, D_FF, D_MODEL), D_FF ** -0.5),
        "final_norm_g": 1.0 + nrm(ks[20], (D_MODEL,), 0.02),
    }


def reference(x_prompt, x_sample, meta_tokens, hg_lb_logits, norm_mix_g, w_in, conv_w, conv_b, rg_wa, rg_ba,
              rg_wx, rg_bx, rg_lambda, hg_norm_g, w_branch_a, w_branch_b, w_out, norm_mlp_g, w_mlp1, w_mlp2,
              final_norm_g):
    y_prompt = encode(x_prompt, meta_tokens, hg_lb_logits, norm_mix_g, w_in, conv_w, conv_b, rg_wa, rg_ba,
                      rg_wx, rg_bx, rg_lambda, hg_norm_g, w_branch_a, w_branch_b, w_out, norm_mlp_g, w_mlp1,
                      w_mlp2, final_norm_g)
    y_sample = encode(x_sample, meta_tokens, hg_lb_logits, norm_mix_g, w_in, conv_w, conv_b, rg_wa, rg_ba,
                      rg_wx, rg_bx, rg_lambda, hg_norm_g, w_branch_a, w_branch_b, w_out, norm_mlp_g, w_mlp1,
                      w_mlp2, final_norm_g)
    return (y_prompt, y_sample)
```

```python
import functools

import numpy as np
import jax
import jax.numpy as jnp
from jax import lax
from jax.experimental import pallas as pl
from jax.experimental.pallas import tpu as pltpu

D_MODEL = 1024
N_META = 16
N_HEADS = 8
D_HEAD = D_MODEL // N_HEADS
D_FF = 4 * D_MODEL
RG_C = 8.0
EPS = 1e-6

SUBLANES = 8
TILE = 128
SEG = TILE // SUBLANES
HALO = SUBLANES
MLP_TILE = 512
FF_CHUNK = 1024
VMEM_LIMIT = 56 * 1024 * 1024

F32 = jnp.float32
BF16 = jnp.bfloat16


def _level_half_sizes(n):
    out, m = [], 1
    while m < n:
        out.append(m)
        m *= 2
    return tuple(out)


LEVELS = _level_half_sizes(TILE)


def _level_masks(n, rev):
    t = np.arange(n)[:, None]
    s = np.arange(n)[None, :]
    masks = [t == s]
    for m in _level_half_sizes(n):
        same = (t // (2 * m)) == (s // (2 * m))
        t_up = (t % (2 * m)) >= m
        s_up = (s % (2 * m)) >= m
        masks.append(same & (~t_up) & s_up if rev else same & t_up & (~s_up))
    return np.stack(masks).astype(np.float32)


def _rmsnorm(x, g):
    ms = jnp.mean(x * x, axis=-1, keepdims=True)
    return x * lax.rsqrt(ms + EPS) * g


def _dot(a, b):
    return jnp.dot(a, b, preferred_element_type=F32)


def _dot_nt(a, b):
    return lax.dot_general(a, b, (((1,), (1,)), ((), ())), preferred_element_type=F32)


def _dot_tn(a, b):
    return lax.dot_general(a, b, (((0,), (0,)), ((), ())), preferred_element_type=F32)


def _gelu_tanh(x):
    c = np.float32(np.sqrt(2.0 / np.pi))
    return 0.5 * x * (1.0 + jnp.tanh(c * (x + 0.044715 * (x * x * x))))


def _silu(x):
    return x * jax.nn.sigmoid(x)


def _lower_bound(lbl):
    mx = jnp.maximum(lbl[0:1, :], lbl[1:2, :])
    e0 = jnp.exp(lbl[0:1, :] - mx)
    e1 = jnp.exp(lbl[1:2, :] - mx)
    return e0 / (e0 + e1)


def _seg_order(n, rev):
    return range(n - 1, -1, -1) if rev else range(n)


def _scan_linear(a_ref, u_ref, carry, rev):
    out = []
    for c in range(N_HEADS):
        h = jnp.zeros((SUBLANES, D_HEAD), F32)
        p = jnp.ones((SUBLANES, D_HEAD), F32)
        for gi in _seg_order(SEG, rev):
            idx = pl.ds(gi, SUBLANES, stride=SEG)
            a = a_ref[c, idx, :]
            h = a * h + u_ref[c, idx, :]
            p = a * p
            u_ref[c, idx, :] = h
            a_ref[c, idx, :] = p
        cc = carry[:, c * D_HEAD:(c + 1) * D_HEAD]
        for r in _seg_order(SUBLANES, rev):
            rows = pl.ds(r * SEG, SEG)
            u_ref[c, rows, :] = u_ref[c, rows, :] + a_ref[c, rows, :] * cc
            cc = p[r:r + 1, :] * cc + h[r:r + 1, :]
        out.append(cc)
    return jnp.concatenate(out, axis=1)


def _scan_sum(g_ref, c, rev):
    h = jnp.zeros((SUBLANES, D_HEAD), F32)
    for gi in _seg_order(SEG, rev):
        idx = pl.ds(gi, SUBLANES, stride=SEG)
        h = h + g_ref[c, idx, :]
        g_ref[c, idx, :] = h
    cc = jnp.zeros((1, D_HEAD), F32)
    for r in _seg_order(SUBLANES, rev):
        rows = pl.ds(r * SEG, SEG)
        g_ref[c, rows, :] = g_ref[c, rows, :] + cc
        cc = cc + h[r:r + 1, :]


def _store_blocks(ref, val):
    for c in range(N_HEADS):
        ref[c] = val[:, c * D_HEAD:(c + 1) * D_HEAD]


def _load_blocks(ref):
    return jnp.concatenate([ref[c] for c in range(N_HEADS)], axis=1)


def _rglru_inputs(xc, rgw_ref, rgb_ref, lam_ref):
    xcb = xc.astype(BF16)
    pre = [_dot(xcb[:, hb * D_HEAD:(hb + 1) * D_HEAD], rgw_ref[hb]) for hb in range(N_HEADS)]
    r_pre = jnp.concatenate([p[:, :D_HEAD] for p in pre], axis=1)
    i_pre = jnp.concatenate([p[:, D_HEAD:] for p in pre], axis=1)
    r = jax.nn.sigmoid(r_pre + rgb_ref[0:1, :])
    i = jax.nn.sigmoid(i_pre + rgb_ref[1:2, :])
    nl = -lam_ref[...]
    softplus = jnp.maximum(nl, 0.0) + jnp.log1p(jnp.exp(-jnp.abs(nl)))
    log_a = (-RG_C * softplus) * r
    a = jnp.exp(log_a)
    th = jnp.tanh(log_a)
    mult = jnp.sqrt(-2.0 * th / (1.0 - th))
    return a, mult * (i * xc)


def _hgrn2_tile(q, k, g, v_ref, b_ref, st_ref, mask_ref, o_ref, rev):
    n = TILE
    row = lax.broadcasted_iota(jnp.int32, (n, D_HEAD), 0)
    edge = 0 if rev else n - 1
    for hd in range(N_HEADS):
        ls = slice(hd * D_HEAD, (hd + 1) * D_HEAD)
        qh, kh, gh = q[:, ls], k[:, ls], g[:, ls]
        b_ref[hd] = gh
        _scan_sum(b_ref, hd, rev)
        b = b_ref[hd]
        b_edge = b_ref[hd, pl.ds(edge, 1), :]
        p = mask_ref[0] * _dot_nt(qh.astype(BF16), kh.astype(BF16))
        for li, m in enumerate(LEVELS):
            pos = row % (2 * m)
            is_q = (pos < m) if rev else (pos >= m)
            if m == 1:
                e = jnp.where(is_q, gh, 0.0)
            elif m == 2:
                g_nxt = pltpu.roll(gh, n - 1, axis=0)
                g_prv = pltpu.roll(gh, 1, axis=0)
                if rev:
                    e = jnp.where(pos == 0, gh + g_nxt, jnp.where(pos == 1, gh, jnp.where(pos == 2, 0.0, g_prv)))
                else:
                    e = jnp.where(pos == 0, g_nxt, jnp.where(pos == 1, 0.0, jnp.where(pos == 2, gh, gh + g_prv)))
            else:
                refs = []
                for c in range(n // (2 * m)):
                    r0 = c * 2 * m + (m if rev else m - 1)
                    refs.append(jnp.broadcast_to(b_ref[hd, pl.ds(r0, 1), :], (2 * m, D_HEAD)))
                rr = jnp.concatenate(refs, axis=0) if len(refs) > 1 else refs[0]
                e = jnp.where(is_q, b - rr, rr - b)
            xl = (jnp.where(is_q, qh, kh) * jnp.exp(e)).astype(BF16)
            p = p + mask_ref[li + 1] * _dot_nt(xl, xl)
        qe = (qh * jnp.exp(b)).astype(BF16)
        ke = (kh * jnp.exp(b_edge - b)).astype(BF16)
        st = st_ref[ls, :]
        vh = v_ref[:, ls]
        o_ref[:, ls] = _dot(p.astype(BF16), vh) + _dot_nt(qe, st.astype(BF16))
        st_ref[ls, :] = st * jnp.exp(b_edge) + _dot_tn(vh, ke)


def _fwd_kernel(x_ref, xn_ref, meta_ref, gmix_ref, w_ref, convw_ref, convb_ref, rgw_ref, rgb_ref, lam_ref,
                lbl_ref, mask_ref,
                xc_ref, q_ref, v_ref, hf_ref, of_ref,
                st_scr, hc_scr, xa_scr, a_scr, u_scr, b_scr):
    j = pl.program_id(1)
    n_real = pl.num_programs(1) - 1
    is_meta = j == 0

    @pl.when(is_meta)
    def _():
        st_scr[...] = jnp.zeros_like(st_scr)
        hc_scr[...] = jnp.zeros_like(hc_scr)
        xa_scr[0:HALO, :] = jnp.zeros((HALO, D_MODEL), F32)

    x_main = jnp.where(is_meta, meta_ref[...], x_ref[...])
    x_next = jnp.where(j == n_real, 0.0, xn_ref[...])
    hx = _rmsnorm(jnp.concatenate([x_main, x_next], axis=0), gmix_ref[...]).astype(BF16)
    xa_scr[HALO:, :] = _dot(hx, w_ref[:, 0:D_MODEL])
    z = _dot(hx[0:TILE, :], w_ref[:, D_MODEL:])
    q = _silu(z[:, 0:D_MODEL])
    ff = z[:, D_MODEL:2 * D_MODEL]
    v_ref[...] = z[:, 2 * D_MODEL:].astype(BF16)
    q_ref[...] = q

    xc = convb_ref[...] + convw_ref[0:1, :] * xa_scr[pl.ds(HALO - 2, TILE), :]
    for tap in range(1, 4):
        xc = xc + convw_ref[tap:tap + 1, :] * xa_scr[pl.ds(HALO - 2 + tap, TILE), :]
    xa_scr[0:HALO, :] = xa_scr[TILE:TILE + HALO, :]
    xc_ref[...] = xc

    a, u = _rglru_inputs(xc, rgw_ref, rgb_ref, lam_ref)
    row = lax.broadcasted_iota(jnp.int32, (TILE, D_MODEL), 0)
    pad = jnp.logical_and(is_meta, row < TILE - N_META)
    _store_blocks(a_scr, a)
    _store_blocks(u_scr, jnp.where(pad, 0.0, u))
    hc_scr[0:1, :] = _scan_linear(a_scr, u_scr, hc_scr[0:1, :], rev=False)
    hf_ref[...] = _load_blocks(u_scr)

    lb = _lower_bound(lbl_ref[...])
    f = lb + (1.0 - lb) * jax.nn.sigmoid(ff)
    _hgrn2_tile(q, 1.0 - f, jnp.log(f), v_ref, b_scr, st_scr, mask_ref, of_ref, rev=False)


def _forward_sweep(x, meta_tile, p):
    bsz, t, _ = x.shape
    nt = t // TILE
    halo_blocks = t // HALO
    row_blk = lambda b, j: (b, jnp.maximum(j - 1, 0), 0)
    const2 = lambda b, j: (0, 0)
    const3 = lambda b, j: (0, 0, 0)
    tok = pl.BlockSpec((None, TILE, D_MODEL), row_blk)
    n_lv = len(LEVELS)
    out_f32 = jax.ShapeDtypeStruct((bsz, t, D_MODEL), F32)
    return pl.pallas_call(
        _fwd_kernel,
        grid=(bsz, nt + 1),
        in_specs=[
            tok,
            pl.BlockSpec((None, HALO, D_MODEL),
                         lambda b, j: (b, jnp.minimum(j * (TILE // HALO), halo_blocks - 1), 0)),
            pl.BlockSpec((TILE, D_MODEL), const2),
            pl.BlockSpec((1, D_MODEL), const2),
            pl.BlockSpec((D_MODEL, 4 * D_MODEL), const2),
            pl.BlockSpec((4, D_MODEL), const2),
            pl.BlockSpec((1, D_MODEL), const2),
            pl.BlockSpec((N_HEADS, D_HEAD, 2 * D_HEAD), const3),
            pl.BlockSpec((2, D_MODEL), const2),
            pl.BlockSpec((1, D_MODEL), const2),
            pl.BlockSpec((2, D_MODEL), const2),
            pl.BlockSpec((n_lv + 1, TILE, TILE), const3),
        ],
        out_specs=[tok, tok, tok, tok, tok],
        out_shape=[out_f32, out_f32, jax.ShapeDtypeStruct((bsz, t, D_MODEL), BF16), out_f32, out_f32],
        scratch_shapes=[
            pltpu.VMEM((D_MODEL, D_HEAD), F32),
            pltpu.VMEM((SUBLANES, D_MODEL), F32),
            pltpu.VMEM((TILE + 2 * HALO, D_MODEL), F32),
            pltpu.VMEM((N_HEADS, TILE, D_HEAD), F32),
            pltpu.VMEM((N_HEADS, TILE, D_HEAD), F32),
            pltpu.VMEM((N_HEADS, TILE, D_HEAD), F32),
        ],
        compiler_params=pltpu.CompilerParams(
            dimension_semantics=("arbitrary", "arbitrary"), vmem_limit_bytes=VMEM_LIMIT),
        name="fwd_sweep",
    )(x, x, meta_tile, p["gmix"], p["w_fwd"], p["conv_w"], p["conv_b"], p["rgw_f"], p["rgb_f"], p["lam_f"],
      p["lbl_f"], p["mask_f"])


def _bwd_kernel(x_ref, xc_ref, q_ref, v_ref, hf_ref, of_ref, gmix_ref, w_ref, rgw_ref, rgb_ref, lam_ref,
                lbl_ref, mask_ref, hgg_ref, wa_ref, wb_ref, wo_ref,
                y_ref,
                st_scr, hc_scr, a_scr, u_scr, b_scr, ob_scr):
    j = pl.program_id(1)

    @pl.when(j == 0)
    def _():
        st_scr[...] = jnp.zeros_like(st_scr)
        hc_scr[...] = jnp.zeros_like(hc_scr)

    x = x_ref[...]
    hx = _rmsnorm(x, gmix_ref[...]).astype(BF16)
    z = _dot(hx, w_ref[...])
    ya = z[:, 0:D_MODEL]
    fb = z[:, D_MODEL:2 * D_MODEL]
    og = z[:, 2 * D_MODEL:3 * D_MODEL]
    gate_a = z[:, 3 * D_MODEL:4 * D_MODEL]
    gate_b = z[:, 4 * D_MODEL:5 * D_MODEL]

    a, u = _rglru_inputs(xc_ref[...], rgw_ref, rgb_ref, lam_ref)
    _store_blocks(a_scr, a)
    _store_blocks(u_scr, u)
    hc_scr[0:1, :] = _scan_linear(a_scr, u_scr, hc_scr[0:1, :], rev=True)
    branch_a = ((hf_ref[...] + _load_blocks(u_scr)) * _gelu_tanh(ya)).astype(BF16)

    lb = _lower_bound(lbl_ref[...])
    f = lb + (1.0 - lb) * jax.nn.sigmoid(fb)
    _hgrn2_tile(q_ref[...], 1.0 - f, jnp.log(f), v_ref, b_scr, st_scr, mask_ref, ob_scr, rev=True)

    o = of_ref[...] + ob_scr[...]
    normed = []
    for hd in range(N_HEADS):
        oh = o[:, hd * D_HEAD:(hd + 1) * D_HEAD]
        normed.append(oh * lax.rsqrt(jnp.mean(oh * oh, axis=-1, keepdims=True) + EPS))
    o = jnp.concatenate(normed, axis=1)
    branch_b = (o * hgg_ref[...] * _silu(og)).astype(BF16)

    merged = (jax.nn.sigmoid(gate_a) * _dot(branch_a, wa_ref[...])
              + jax.nn.sigmoid(gate_b) * _dot(branch_b, wb_ref[...]))
    y_ref[...] = x + _dot(merged.astype(BF16), wo_ref[...])


def _backward_sweep(x, xc, q, v, hf, of, p):
    bsz, t, _ = x.shape
    nt = t // TILE
    row_blk = lambda b, j: (b, nt - 1 - j, 0)
    const2 = lambda b, j: (0, 0)
    const3 = lambda b, j: (0, 0, 0)
    tok = pl.BlockSpec((None, TILE, D_MODEL), row_blk)
    sq = pl.BlockSpec((D_MODEL, D_MODEL), const2)
    vec = pl.BlockSpec((1, D_MODEL), const2)
    n_lv = len(LEVELS)
    return pl.pallas_call(
        _bwd_kernel,
        grid=(bsz, nt),
        in_specs=[
            tok, tok, tok, tok, tok, tok,
            vec,
            pl.BlockSpec((D_MODEL, 5 * D_MODEL), const2),
            pl.BlockSpec((N_HEADS, D_HEAD, 2 * D_HEAD), const3),
            pl.BlockSpec((2, D_MODEL), const2),
            vec,
            pl.BlockSpec((2, D_MODEL), const2),
            pl.BlockSpec((n_lv + 1, TILE, TILE), const3),
            vec, sq, sq, sq,
        ],
        out_specs=tok,
        out_shape=jax.ShapeDtypeStruct((bsz, t, D_MODEL), F32),
        scratch_shapes=[
            pltpu.VMEM((D_MODEL, D_HEAD), F32),
            pltpu.VMEM((SUBLANES, D_MODEL), F32),
            pltpu.VMEM((N_HEADS, TILE, D_HEAD), F32),
            pltpu.VMEM((N_HEADS, TILE, D_HEAD), F32),
            pltpu.VMEM((N_HEADS, TILE, D_HEAD), F32),
            pltpu.VMEM((TILE, D_MODEL), F32),
        ],
        compiler_params=pltpu.CompilerParams(
            dimension_semantics=("arbitrary", "arbitrary"), vmem_limit_bytes=VMEM_LIMIT),
        name="bwd_sweep",
    )(x, xc, q, v, hf, of, p["gmix"], p["w_bwd"], p["rgw_b"], p["rgb_b"], p["lam_b"], p["lbl_b"],
      p["mask_b"], p["hgg"], p["wa"], p["wb"], p["wo"])


def _mlp_kernel(x_ref, gmlp_ref, w1_ref, w2_ref, gfin_ref, y_ref):
    x = x_ref[...]
    hx = _rmsnorm(x, gmlp_ref[...]).astype(BF16)
    acc = x
    for c in range(D_FF // FF_CHUNK):
        cols = slice(c * FF_CHUNK, (c + 1) * FF_CHUNK)
        hm = jnp.maximum(_dot(hx, w1_ref[:, cols]), 0.0)
        acc = acc + _dot((hm * hm).astype(BF16), w2_ref[cols, :])
    y_ref[...] = _rmsnorm(acc, gfin_ref[...])


def _channel_mixer(x, p):
    bsz, t, _ = x.shape
    rows = bsz * t
    x2 = x.reshape(rows, D_MODEL)
    tok = pl.BlockSpec((MLP_TILE, D_MODEL), lambda i: (i, 0))
    const2 = lambda i: (0, 0)
    vec = pl.BlockSpec((1, D_MODEL), const2)
    y = pl.pallas_call(
        _mlp_kernel,
        grid=(rows // MLP_TILE,),
        in_specs=[tok, vec, pl.BlockSpec((D_MODEL, D_FF), const2), pl.BlockSpec((D_FF, D_MODEL), const2), vec],
        out_specs=tok,
        out_shape=jax.ShapeDtypeStruct((rows, D_MODEL), F32),
        compiler_params=pltpu.CompilerParams(
            dimension_semantics=("arbitrary",), vmem_limit_bytes=VMEM_LIMIT),
        name="channel_mixer",
    )(x2, p["gmlp"], p["w_mlp1"], p["w_mlp2"], p["gfin"])
    return y.reshape(bsz, t, D_MODEL)


def _prepare_params(meta_tokens, hg_lb_logits, norm_mix_g, w_in, conv_w, conv_b, rg_wa, rg_ba, rg_wx, rg_bx,
                    rg_lambda, hg_norm_g, w_branch_a, w_branch_b, w_out, norm_mlp_g, w_mlp1, w_mlp2,
                    final_norm_g):
    d = D_MODEL
    w = w_in[0].astype(BF16)
    col = lambda i: w[:, i * d:(i + 1) * d]
    row = lambda a: a.reshape(1, d).astype(F32)
    rgw = lambda k: jnp.concatenate([rg_wa[0, k], rg_wx[0, k]], axis=-1).astype(BF16)
    rgb = lambda k: jnp.stack([rg_ba[0, k], rg_bx[0, k]]).astype(F32)
    meta_tile = jnp.concatenate(
        [jnp.zeros((TILE - N_META, d), F32), meta_tokens.astype(F32)], axis=0)
    p = dict(
        gmix=row(norm_mix_g[0]),
        w_fwd=jnp.concatenate([col(0), col(2), col(3), col(5)], axis=1),
        w_bwd=jnp.concatenate([col(1), col(4), col(6), col(7), col(8)], axis=1),
        conv_w=conv_w[0].astype(F32), conv_b=row(conv_b[0]),
        rgw_f=rgw(0), rgw_b=rgw(1), rgb_f=rgb(0), rgb_b=rgb(1),
        lam_f=row(rg_lambda[0, 0]), lam_b=row(rg_lambda[0, 1]),
        lbl_f=hg_lb_logits[:, 0, :].astype(F32), lbl_b=hg_lb_logits[:, 1, :].astype(F32),
        mask_f=jnp.asarray(_level_masks(TILE, False)), mask_b=jnp.asarray(_level_masks(TILE, True)),
        hgg=row(hg_norm_g[0]),
        wa=w_branch_a[0].astype(BF16), wb=w_branch_b[0].astype(BF16), wo=w_out[0].astype(BF16),
        gmlp=row(norm_mlp_g[0]), w_mlp1=w_mlp1[0].astype(BF16), w_mlp2=w_mlp2[0].astype(BF16),
        gfin=row(final_norm_g),
    )
    return meta_tile, p


def _encode(x, meta_tile, p):
    assert x.shape[1] % TILE == 0 and (x.shape[0] * x.shape[1]) % MLP_TILE == 0
    xc, q, v, hf, of = _forward_sweep(x, meta_tile, p)
    x_mid = _backward_sweep(x, xc, q, v, hf, of, p)
    return _channel_mixer(x_mid, p)


def kernel(x_prompt, x_sample, meta_tokens, hg_lb_logits, norm_mix_g, w_in, conv_w, conv_b, rg_wa, rg_ba, rg_wx,
           rg_bx, rg_lambda, hg_norm_g, w_branch_a, w_branch_b, w_out, norm_mlp_g, w_mlp1, w_mlp2, final_norm_g):
    meta_tile, p = _prepare_params(meta_tokens, hg_lb_logits, norm_mix_g, w_in, conv_w, conv_b, rg_wa, rg_ba,
                                   rg_wx, rg_bx, rg_lambda, hg_norm_g, w_branch_a, w_branch_b, w_out,
                                   norm_mlp_g, w_mlp1, w_mlp2, final_norm_g)
    return (_encode(x_prompt, meta_tile, p), _encode(x_sample, meta_tile, p))
```

```python
import numpy as np
import jax
import jax.numpy as jnp
from jax import lax
from jax.experimental import pallas as pl
from jax.experimental.pallas import tpu as pltpu

D_MODEL = 1024
N_META = 16
N_HEADS = 8
D_HEAD = D_MODEL // N_HEADS
D_FF = 4 * D_MODEL
RG_C = 8.0
EPS = 1e-6
LOG2_E = float(np.log2(np.e))

SUBLANES = 8
BF16_ROWS = 16
TILE = 128
SEG = TILE // SUBLANES
HALO = SUBLANES
MLP_TILE = 512
FF_CHUNK = 1024
Z_CHUNK = 256
VMEM_LIMIT = 56 * 1024 * 1024

F32 = jnp.float32
BF16 = jnp.bfloat16


def _level_half_sizes(n):
    out, m = [], 1
    while m < n:
        out.append(m)
        m *= 2
    return tuple(out)


LEVELS = _level_half_sizes(TILE)
FINE_LEVELS = tuple(m for m in LEVELS if m < SUBLANES)


def _fine_masks(n, rev):
    t = np.arange(n)[:, None]
    s = np.arange(n)[None, :]
    masks = [t == s]
    for m in FINE_LEVELS:
        same = (t // (2 * m)) == (s // (2 * m))
        t_up = (t % (2 * m)) >= m
        s_up = (s % (2 * m)) >= m
        masks.append(same & (~t_up) & s_up if rev else same & t_up & (~s_up))
    return np.stack(masks).astype(np.float32)


def _rmsnorm(x, g):
    ms = jnp.mean(x * x, axis=-1, keepdims=True)
    return x * lax.rsqrt(ms + EPS) * g


def _dot(a, b):
    return jnp.dot(a, b, preferred_element_type=F32)


def _dot_nt(a, b):
    return lax.dot_general(a, b, (((1,), (1,)), ((), ())), preferred_element_type=F32)


def _dot_tn(a, b):
    return lax.dot_general(a, b, (((0,), (0,)), ((), ())), preferred_element_type=F32)


def _gelu_tanh(x):
    c = np.float32(np.sqrt(2.0 / np.pi))
    return 0.5 * x * (1.0 + jnp.tanh(c * (x + 0.044715 * (x * x * x))))


def _sigmoid(x):
    return 0.5 * jnp.tanh(0.5 * x) + 0.5


def _silu(x):
    hx = 0.5 * x
    return hx * jnp.tanh(hx) + hx


def _lower_bound(lbl):
    mx = jnp.maximum(lbl[0:1, :], lbl[1:2, :])
    e0 = jnp.exp(lbl[0:1, :] - mx)
    e1 = jnp.exp(lbl[1:2, :] - mx)
    return e0 / (e0 + e1)


def _lanes(c):
    return slice(c * D_HEAD, (c + 1) * D_HEAD)


def _seg_order(n, rev):
    return range(n - 1, -1, -1) if rev else range(n)


def _scan_linear(a_ref, u_ref, carry, rev):
    hs = [jnp.zeros((SUBLANES, D_HEAD), F32)] * N_HEADS
    ps = [jnp.ones((SUBLANES, D_HEAD), F32)] * N_HEADS
    for gi in _seg_order(SEG, rev):
        idx = pl.ds(gi, SUBLANES, stride=SEG)
        for c in range(N_HEADS):
            a = a_ref[c, idx, :]
            hs[c] = a * hs[c] + u_ref[c, idx, :]
            ps[c] = a * ps[c]
            u_ref[c, idx, :] = hs[c]
            a_ref[c, idx, :] = ps[c]
    cs = [carry[:, _lanes(c)] for c in range(N_HEADS)]
    for r in _seg_order(SUBLANES, rev):
        rows = pl.ds(r * SEG, SEG)
        for c in range(N_HEADS):
            u_ref[c, rows, :] = u_ref[c, rows, :] + a_ref[c, rows, :] * cs[c]
            cs[c] = ps[c][r:r + 1, :] * cs[c] + hs[c][r:r + 1, :]
    return jnp.concatenate(cs, axis=1)


def _scan_sum(g_ref, rev):
    hs = [jnp.zeros((SUBLANES, D_HEAD), F32)] * N_HEADS
    for gi in _seg_order(SEG, rev):
        idx = pl.ds(gi, SUBLANES, stride=SEG)
        for c in range(N_HEADS):
            hs[c] = hs[c] + g_ref[c, idx, :]
            g_ref[c, idx, :] = hs[c]
    cs = [jnp.zeros((1, D_HEAD), F32)] * N_HEADS
    for r in _seg_order(SUBLANES, rev):
        rows = pl.ds(r * SEG, SEG)
        for c in range(N_HEADS):
            g_ref[c, rows, :] = g_ref[c, rows, :] + cs[c]
            cs[c] = cs[c] + hs[c][r:r + 1, :]


def _rglru_coef(lam_ref):
    nl = -lam_ref[...]
    return -RG_C * (jnp.maximum(nl, 0.0) + jnp.log1p(jnp.exp(-jnp.abs(nl))))


def _rglru_block(xc, c, rgw_ref, rgb_ref, coef):
    pre = _dot(xc.astype(BF16), rgw_ref[c])
    r = _sigmoid(pre[:, :D_HEAD] + rgb_ref[0:1, _lanes(c)])
    i = _sigmoid(pre[:, D_HEAD:] + rgb_ref[1:2, _lanes(c)])
    log_a = coef[:, _lanes(c)] * r
    a = jnp.exp(log_a)
    th = jnp.tanh(log_a)
    sq = -2.0 * th / (1.0 - th)
    mult = jnp.where(sq > 0.0, sq * lax.rsqrt(sq), 0.0)
    return a, mult * (i * xc)


def _hgrn2_tile(q_ref, k_ref, g_ref, b_ref, v_ref, p_ref, x_ref, qe_ref, ke_ref, st_ref, mask_ref, o_ref, rev,
                side_work=()):
    n = TILE
    heads = range(N_HEADS)
    edge = 0 if rev else n - 1
    sub = lax.broadcasted_iota(jnp.int32, (n, D_HEAD), 0) % SUBLANES

    def boundary(hd, m):
        parts = []
        for c in range(n // (2 * m)):
            r0 = c * 2 * m + (m if rev else m - 1)
            parts.append(jnp.broadcast_to(b_ref[hd, pl.ds(r0, 1), :], (2 * m, D_HEAD)))
        return jnp.concatenate(parts, axis=0) if len(parts) > 1 else parts[0]

    _scan_sum(b_ref, rev)

    for hd in heads:
        qh = q_ref[:,_lanes(hd)]
        kh = k_ref[hd]
        b = b_ref[hd]
        b_edge = b_ref[hd, pl.ds(edge, 1), :]
        x_ref[2, hd] = qh.astype(BF16)
        x_ref[3, hd] = kh.astype(BF16)
        qe_ref[hd] = (qh * jnp.exp2(b)).astype(BF16)
        ke_ref[hd] = (kh * jnp.exp2(b_edge - b)).astype(BF16)
    for hd in heads:
        p_ref[hd] = mask_ref[0] * _dot_nt(x_ref[2, hd], x_ref[3, hd])

    for li, m in enumerate(LEVELS):
        if li < len(side_work):
            side_work[li]()
        buf = li % 2
        is_q_blk = lambda jb: (jb % 2 == 1) != rev
        for hd in heads:
            qh = q_ref[:,_lanes(hd)]
            kh = k_ref[hd]
            if m < SUBLANES:
                pos = sub % (2 * m)
                is_q = (pos < m) if rev else (pos >= m)
                if m == 1:
                    e = jnp.where(is_q, g_ref[hd], 0.0)
                elif m == 2:
                    gh = g_ref[hd]
                    g_nxt = pltpu.roll(gh, n - 1, axis=0)
                    g_prv = pltpu.roll(gh, 1, axis=0)
                    if rev:
                        e = jnp.where(pos == 0, gh + g_nxt,
                                      jnp.where(pos == 1, gh, jnp.where(pos == 2, 0.0, g_prv)))
                    else:
                        e = jnp.where(pos == 0, g_nxt,
                                      jnp.where(pos == 1, 0.0, jnp.where(pos == 2, gh, gh + g_prv)))
                else:
                    e = -jnp.abs(b_ref[hd] - boundary(hd, m))
                sel = jnp.where(is_q, qh, kh)
            else:
                e = -jnp.abs(b_ref[hd] - boundary(hd, m))
                sel = jnp.concatenate(
                    [(qh if is_q_blk(jb) else kh)[jb * m:(jb + 1) * m] for jb in range(n // m)], axis=0)
            x_ref[buf, hd] = (sel * jnp.exp2(e)).astype(BF16)
        for hd in heads:
            if m < SUBLANES:
                xl = x_ref[buf, hd]
                p_ref[hd] = p_ref[hd] + mask_ref[1 + li] * _dot_nt(xl, xl)
                continue
            groups = range(n // (2 * m))
            q_rows = [slice(c * 2 * m + (0 if rev else m), c * 2 * m + (m if rev else 2 * m)) for c in groups]
            k_cols = [slice(c * 2 * m + (m if rev else 0), c * 2 * m + (2 * m if rev else m)) for c in groups]
            if m >= BF16_ROWS:
                lhs = jnp.concatenate([x_ref[buf, hd, rs, :] for rs in q_rows], axis=0) if len(q_rows) > 1 \
                    else x_ref[buf, hd, q_rows[0], :]
                s = _dot_nt(lhs, x_ref[buf, hd])
                s_rows = [slice(c * m, (c + 1) * m) for c in groups]
            else:
                xl = x_ref[buf, hd]
                s = _dot_nt(xl, xl)
                s_rows = q_rows
            for rs, sr, kc in zip(q_rows, s_rows, k_cols):
                p_ref[hd, rs, kc] = s[sr, kc]

    for hd in heads:
        st = st_ref[_lanes(hd), :]
        vh = v_ref[:,_lanes(hd)]
        b_edge = b_ref[hd, pl.ds(edge, 1), :]
        o_ref[:, _lanes(hd)] = (_dot(p_ref[hd].astype(BF16), vh)
                                + _dot_nt(qe_ref[hd], st.astype(BF16)))
        st_ref[_lanes(hd), :] = st * jnp.exp2(b_edge) + _dot_tn(vh, ke_ref[hd])


def _hgrn2_scratch():
    blk = (N_HEADS, TILE, D_HEAD)
    return [
        pltpu.VMEM(blk, F32),
        pltpu.VMEM(blk, F32),
        pltpu.VMEM(blk, F32),
        pltpu.VMEM((N_HEADS, TILE, TILE), F32),
        pltpu.VMEM((4,) + blk, BF16),
        pltpu.VMEM(blk, BF16),
        pltpu.VMEM(blk, BF16),
    ]


def _forget_gate_block(ff, lb, c, k_ref, g_ref, b_ref):
    f = lb[:, _lanes(c)] + (1.0 - lb[:, _lanes(c)]) * _sigmoid(ff)
    g = jnp.log(f) * LOG2_E
    k_ref[c] = 1.0 - f
    g_ref[c] = g
    b_ref[c] = g


def _fwd_kernel(x_ref, xn_ref, meta_ref, gmix_ref, w_ref, convw_ref, convb_ref, rgw_ref, rgb_ref, lam_ref,
                lbl_ref, mask_ref,
                xc_ref, q_ref, v_ref, hf_ref, of_ref,
                st_scr, hc_scr, xa_scr, a_scr, u_scr, z_scr, k_scr, g_scr, b_scr, p_scr, x_scr, qe_scr, ke_scr):
    j = pl.program_id(1)
    n_real = pl.num_programs(1) - 1
    is_meta = j == 0

    @pl.when(is_meta)
    def _():
        st_scr[...] = jnp.zeros_like(st_scr)
        hc_scr[...] = jnp.zeros_like(hc_scr)
        xa_scr[:, 0:HALO, :] = jnp.zeros((N_HEADS, HALO, D_HEAD), F32)

    x_main = jnp.where(is_meta, meta_ref[...], x_ref[...])
    x_next = jnp.where(j == n_real, 0.0, xn_ref[...])
    hx = _rmsnorm(jnp.concatenate([x_main, x_next], axis=0), gmix_ref[...]).astype(BF16)
    hx_main = hx[0:TILE, :]

    for c in range(N_HEADS):
        ls = _lanes(c)
        if c % 2 == 0:
            xa = _dot(hx, w_ref[:, c * D_HEAD:(c + 2) * D_HEAD])
        xa_scr[c, HALO:, :] = xa[:, (c % 2) * D_HEAD:(c % 2 + 1) * D_HEAD]
        shifted = [xa_scr[c, pl.ds(HALO - 2 + i, SUBLANES, stride=SEG), :] for i in range(SEG + 3)]
        for gi in range(SEG):
            acc = convb_ref[:, ls] + convw_ref[0:1, ls] * shifted[gi]
            for tap in range(1, 4):
                acc = acc + convw_ref[tap:tap + 1, ls] * shifted[gi + tap]
            a_scr[c, pl.ds(gi, SUBLANES, stride=SEG), :] = acc
        xa_scr[c, 0:HALO, :] = xa_scr[c, TILE:TILE + HALO, :]

    coef = _rglru_coef(lam_ref)
    row = lax.broadcasted_iota(jnp.int32, (TILE, D_HEAD), 0)
    pad = jnp.logical_and(is_meta, row < TILE - N_META)
    for c in range(N_HEADS):
        xc = a_scr[c]
        xc_ref[:, _lanes(c)] = xc
        a, u = _rglru_block(xc, c, rgw_ref, rgb_ref, coef)
        a_scr[c] = a
        u_scr[c] = jnp.where(pad, 0.0, u)
        if c % 2 == 1:
            i = c // 2
            for part in range(3):
                lo = part * D_MODEL + i * Z_CHUNK
                z_scr[:, lo:lo + Z_CHUNK] = _dot(hx_main, w_ref[:, D_MODEL + lo:D_MODEL + lo + Z_CHUNK])
    hc_scr[0:1, :] = _scan_linear(a_scr, u_scr, hc_scr[0:1, :], rev=False)
    for c in range(N_HEADS):
        hf_ref[:, _lanes(c)] = u_scr[c]

    lb = _lower_bound(lbl_ref[...])
    for c in range(N_HEADS):
        q_ref[:, _lanes(c)] = _silu(z_scr[:, _lanes(c)])
        _forget_gate_block(z_scr[:, D_MODEL + c * D_HEAD:D_MODEL + (c + 1) * D_HEAD], lb, c, k_scr, g_scr, b_scr)
        v_ref[:, _lanes(c)] = z_scr[:, 2 * D_MODEL + c * D_HEAD:2 * D_MODEL + (c + 1) * D_HEAD].astype(BF16)

    _hgrn2_tile(q_ref, k_scr, g_scr, b_scr, v_ref, p_scr, x_scr, qe_scr, ke_scr, st_scr, mask_ref, of_ref,
                rev=False)


def _forward_sweep(x, meta_tile, p):
    bsz, t, _ = x.shape
    nt = t // TILE
    halo_blocks = t // HALO
    row_blk = lambda b, j: (b, jnp.maximum(j - 1, 0), 0)
    const2 = lambda b, j: (0, 0)
    const3 = lambda b, j: (0, 0, 0)
    tok = pl.BlockSpec((None, TILE, D_MODEL), row_blk)
    blk = (N_HEADS, TILE, D_HEAD)
    out_f32 = jax.ShapeDtypeStruct((bsz, t, D_MODEL), F32)
    return pl.pallas_call(
        _fwd_kernel,
        grid=(bsz, nt + 1),
        in_specs=[
            tok,
            pl.BlockSpec((None, HALO, D_MODEL),
                         lambda b, j: (b, jnp.minimum(j * (TILE // HALO), halo_blocks - 1), 0)),
            pl.BlockSpec((TILE, D_MODEL), const2),
            pl.BlockSpec((1, D_MODEL), const2),
            pl.BlockSpec((D_MODEL, 4 * D_MODEL), const2),
            pl.BlockSpec((4, D_MODEL), const2),
            pl.BlockSpec((1, D_MODEL), const2),
            pl.BlockSpec((N_HEADS, D_HEAD, 2 * D_HEAD), const3),
            pl.BlockSpec((2, D_MODEL), const2),
            pl.BlockSpec((1, D_MODEL), const2),
            pl.BlockSpec((2, D_MODEL), const2),
            pl.BlockSpec((1 + len(FINE_LEVELS), TILE, TILE), const3),
        ],
        out_specs=[tok, tok, tok, tok, tok],
        out_shape=[out_f32, out_f32, jax.ShapeDtypeStruct((bsz, t, D_MODEL), BF16), out_f32, out_f32],
        scratch_shapes=[
            pltpu.VMEM((D_MODEL, D_HEAD), F32),
            pltpu.VMEM((SUBLANES, D_MODEL), F32),
            pltpu.VMEM((N_HEADS, TILE + 2 * HALO, D_HEAD), F32),
            pltpu.VMEM(blk, F32),
            pltpu.VMEM(blk, F32),
            pltpu.VMEM((TILE, 3 * D_MODEL), F32),
        ] + _hgrn2_scratch(),
        compiler_params=pltpu.CompilerParams(
            dimension_semantics=("arbitrary", "arbitrary"), vmem_limit_bytes=VMEM_LIMIT),
        name="fwd_sweep",
    )(x, x, meta_tile, p["gmix"], p["w_fwd"], p["conv_w"], p["conv_b"], p["rgw_f"], p["rgb_f"], p["lam_f"],
      p["lbl_f"], p["mask_f"])


def _bwd_kernel(x_ref, xc_ref, q_ref, v_ref, hf_ref, of_ref, gmix_ref, w_ref, rgw_ref, rgb_ref, lam_ref,
                lbl_ref, mask_ref, hgg_ref, wa_ref, wb_ref, wo_ref,
                y_ref,
                st_scr, hc_scr, z_scr, a_scr, u_scr, ob_scr, ma_scr, ba_scr, bb_scr,
                k_scr, g_scr, b_scr, p_scr, x_scr, qe_scr, ke_scr):
    j = pl.program_id(1)

    @pl.when(j == 0)
    def _():
        st_scr[...] = jnp.zeros_like(st_scr)
        hc_scr[...] = jnp.zeros_like(hc_scr)

    hx = _rmsnorm(x_ref[...], gmix_ref[...]).astype(BF16)

    coef = _rglru_coef(lam_ref)
    for c in range(N_HEADS):
        a, u = _rglru_block(xc_ref[:, _lanes(c)], c, rgw_ref, rgb_ref, coef)
        a_scr[c] = a
        u_scr[c] = u
        if c % 2 == 1:
            i = c // 2
            for part in range(5):
                lo = part * D_MODEL + i * Z_CHUNK
                z_scr[:, lo:lo + Z_CHUNK] = _dot(hx, w_ref[:, lo:lo + Z_CHUNK])
    hc_scr[0:1, :] = _scan_linear(a_scr, u_scr, hc_scr[0:1, :], rev=True)

    lb = _lower_bound(lbl_ref[...])
    for c in range(N_HEADS):
        ls = _lanes(c)
        _forget_gate_block(z_scr[:, D_MODEL + c * D_HEAD:D_MODEL + (c + 1) * D_HEAD], lb, c, k_scr, g_scr, b_scr)
        ba_scr[:, ls] = ((hf_ref[:, ls] + u_scr[c]) * _gelu_tanh(z_scr[:, ls])).astype(BF16)

    def branch_a_slice(i):
        def run():
            cols = slice(i * Z_CHUNK, (i + 1) * Z_CHUNK)
            gate = _sigmoid(z_scr[:, 3 * D_MODEL + i * Z_CHUNK:3 * D_MODEL + (i + 1) * Z_CHUNK])
            ma_scr[:, cols] = gate * _dot(ba_scr[...], wa_ref[:, cols])
        return run

    _hgrn2_tile(q_ref, k_scr, g_scr, b_scr, v_ref, p_scr, x_scr, qe_scr, ke_scr, st_scr, mask_ref, ob_scr,
                rev=True, side_work=[branch_a_slice(i) for i in range(D_MODEL // Z_CHUNK)])

    for c in range(N_HEADS):
        ls = _lanes(c)
        o = of_ref[:, ls] + ob_scr[:, ls]
        o = o * lax.rsqrt(jnp.mean(o * o, axis=-1, keepdims=True) + EPS)
        og = z_scr[:, 2 * D_MODEL + c * D_HEAD:2 * D_MODEL + (c + 1) * D_HEAD]
        bb_scr[:, ls] = (o * hgg_ref[:, ls] * _silu(og)).astype(BF16)

    merged = ma_scr[...] + _sigmoid(z_scr[:, 4 * D_MODEL:5 * D_MODEL]) * _dot(bb_scr[...], wb_ref[...])
    y_ref[...] = x_ref[...] + _dot(merged.astype(BF16), wo_ref[...])


def _backward_sweep(x, xc, q, v, hf, of, p):
    bsz, t, _ = x.shape
    nt = t // TILE
    row_blk = lambda b, j: (b, nt - 1 - j, 0)
    const2 = lambda b, j: (0, 0)
    const3 = lambda b, j: (0, 0, 0)
    tok = pl.BlockSpec((None, TILE, D_MODEL), row_blk)
    sq = pl.BlockSpec((D_MODEL, D_MODEL), const2)
    vec = pl.BlockSpec((1, D_MODEL), const2)
    blk = (N_HEADS, TILE, D_HEAD)
    return pl.pallas_call(
        _bwd_kernel,
        grid=(bsz, nt),
        in_specs=[
            tok, tok, tok, tok, tok, tok,
            vec,
            pl.BlockSpec((D_MODEL, 5 * D_MODEL), const2),
            pl.BlockSpec((N_HEADS, D_HEAD, 2 * D_HEAD), const3),
            pl.BlockSpec((2, D_MODEL), const2),
            vec,
            pl.BlockSpec((2, D_MODEL), const2),
            pl.BlockSpec((1 + len(FINE_LEVELS), TILE, TILE), const3),
            vec, sq, sq, sq,
        ],
        out_specs=tok,
        out_shape=jax.ShapeDtypeStruct((bsz, t, D_MODEL), F32),
        scratch_shapes=[
            pltpu.VMEM((D_MODEL, D_HEAD), F32),
            pltpu.VMEM((SUBLANES, D_MODEL), F32),
            pltpu.VMEM((TILE, 5 * D_MODEL), F32),
            pltpu.VMEM(blk, F32),
            pltpu.VMEM(blk, F32),
            pltpu.VMEM((TILE, D_MODEL), F32),
            pltpu.VMEM((TILE, D_MODEL), F32),
            pltpu.VMEM((TILE, D_MODEL), BF16),
            pltpu.VMEM((TILE, D_MODEL), BF16),
        ] + _hgrn2_scratch(),
        compiler_params=pltpu.CompilerParams(
            dimension_semantics=("arbitrary", "arbitrary"), vmem_limit_bytes=VMEM_LIMIT),
        name="bwd_sweep",
    )(x, xc, q, v, hf, of, p["gmix"], p["w_bwd"], p["rgw_b"], p["rgb_b"], p["lam_b"], p["lbl_b"],
      p["mask_b"], p["hgg"], p["wa"], p["wb"], p["wo"])


def _mlp_kernel(x_ref, gmlp_ref, w1_ref, w2_ref, gfin_ref, y_ref):
    x = x_ref[...]
    hx = _rmsnorm(x, gmlp_ref[...]).astype(BF16)
    acc = x
    for c in range(D_FF // FF_CHUNK):
        cols = slice(c * FF_CHUNK, (c + 1) * FF_CHUNK)
        hm = jnp.maximum(_dot(hx, w1_ref[:, cols]), 0.0)
        acc = acc + _dot((hm * hm).astype(BF16), w2_ref[cols, :])
    y_ref[...] = _rmsnorm(acc, gfin_ref[...])


def _channel_mixer(x, p):
    bsz, t, _ = x.shape
    rows = bsz * t
    x2 = x.reshape(rows, D_MODEL)
    tok = pl.BlockSpec((MLP_TILE, D_MODEL), lambda i: (i, 0))
    const2 = lambda i: (0, 0)
    vec = pl.BlockSpec((1, D_MODEL), const2)
    y = pl.pallas_call(
        _mlp_kernel,
        grid=(rows // MLP_TILE,),
        in_specs=[tok, vec, pl.BlockSpec((D_MODEL, D_FF), const2), pl.BlockSpec((D_FF, D_MODEL), const2), vec],
        out_specs=tok,
        out_shape=jax.ShapeDtypeStruct((rows, D_MODEL), F32),
        compiler_params=pltpu.CompilerParams(
            dimension_semantics=("arbitrary",), vmem_limit_bytes=VMEM_LIMIT),
        name="channel_mixer",
    )(x2, p["gmlp"], p["w_mlp1"], p["w_mlp2"], p["gfin"])
    return y.reshape(bsz, t, D_MODEL)


def _prepare_params(meta_tokens, hg_lb_logits, norm_mix_g, w_in, conv_w, conv_b, rg_wa, rg_ba, rg_wx, rg_bx,
                    rg_lambda, hg_norm_g, w_branch_a, w_branch_b, w_out, norm_mlp_g, w_mlp1, w_mlp2,
                    final_norm_g):
    d = D_MODEL
    w = w_in[0].astype(BF16)
    col = lambda i: w[:, i * d:(i + 1) * d]
    row = lambda a: a.reshape(1, d).astype(F32)
    rgw = lambda k: jnp.concatenate([rg_wa[0, k], rg_wx[0, k]], axis=-1).astype(BF16)
    rgb = lambda k: jnp.stack([rg_ba[0, k], rg_bx[0, k]]).astype(F32)
    meta_tile = jnp.concatenate(
        [jnp.zeros((TILE - N_META, d), F32), meta_tokens.astype(F32)], axis=0)
    p = dict(
        gmix=row(norm_mix_g[0]),
        w_fwd=jnp.concatenate([col(0), col(2), col(3), col(5)], axis=1),
        w_bwd=jnp.concatenate([col(1), col(4), col(6), col(7), col(8)], axis=1),
        conv_w=conv_w[0].astype(F32), conv_b=row(conv_b[0]),
        rgw_f=rgw(0), rgw_b=rgw(1), rgb_f=rgb(0), rgb_b=rgb(1),
        lam_f=row(rg_lambda[0, 0]), lam_b=row(rg_lambda[0, 1]),
        lbl_f=hg_lb_logits[:, 0, :].astype(F32), lbl_b=hg_lb_logits[:, 1, :].astype(F32),
        mask_f=jnp.asarray(_fine_masks(TILE, False)), mask_b=jnp.asarray(_fine_masks(TILE, True)),
        hgg=row(hg_norm_g[0]),
        wa=w_branch_a[0].astype(BF16), wb=w_branch_b[0].astype(BF16), wo=w_out[0].astype(BF16),
        gmlp=row(norm_mlp_g[0]), w_mlp1=w_mlp1[0].astype(BF16), w_mlp2=w_mlp2[0].astype(BF16),
        gfin=row(final_norm_g),
    )
    return meta_tile, p


def _encode(x, meta_tile, p):
    assert x.shape[1] % TILE == 0 and (x.shape[0] * x.shape[1]) % MLP_TILE == 0
    xc, q, v, hf, of = _forward_sweep(x, meta_tile, p)
    x_mid = _backward_sweep(x, xc, q, v, hf, of, p)
    return _channel_mixer(x_mid, p)


def kernel(x_prompt, x_sample, meta_tokens, hg_lb_logits, norm_mix_g, w_in, conv_w, conv_b, rg_wa, rg_ba, rg_wx,
           rg_bx, rg_lambda, hg_norm_g, w_branch_a, w_branch_b, w_out, norm_mlp_g, w_mlp1, w_mlp2, final_norm_g):
    meta_tile, p = _prepare_params(meta_tokens, hg_lb_logits, norm_mix_g, w_in, conv_w, conv_b, rg_wa, rg_ba,
                                   rg_wx, rg_bx, rg_lambda, hg_norm_g, w_branch_a, w_branch_b, w_out,
                                   norm_mlp_g, w_mlp1, w_mlp2, final_norm_g)
    return (_encode(x_prompt, meta_tile, p), _encode(x_sample, meta_tile, p))
```

```python
import functools

import numpy as np
import jax
import jax.numpy as jnp
from jax import lax
from jax.experimental import pallas as pl
from jax.experimental.pallas import tpu as pltpu

D_MODEL = 1024
N_META = 16
N_HEADS = 8
D_HEAD = D_MODEL // N_HEADS
D_FF = 4 * D_MODEL
RG_C = 8.0
EPS = 1e-6
LOG2_E = float(np.log2(np.e))

SUBLANES = 8
BF16_ROWS = 16
TILE = 128
SEG = TILE // SUBLANES
HALO = SUBLANES
MLP_TILE = 512
FF_CHUNK = 1024
Z_CHUNK = 256
VMEM_LIMIT = 56 * 1024 * 1024

F32 = jnp.float32
BF16 = jnp.bfloat16


def _level_half_sizes(n):
    out, m = [], 1
    while m < n:
        out.append(m)
        m *= 2
    return tuple(out)


LEVELS = _level_half_sizes(TILE)
FINE_LEVELS = tuple(m for m in LEVELS if m < SUBLANES)


def _fine_masks(n, rev):
    t = np.arange(n)[:, None]
    s = np.arange(n)[None, :]
    masks = [t == s]
    for m in FINE_LEVELS:
        same = (t // (2 * m)) == (s // (2 * m))
        t_up = (t % (2 * m)) >= m
        s_up = (s % (2 * m)) >= m
        masks.append(same & (~t_up) & s_up if rev else same & t_up & (~s_up))
    return np.stack(masks).astype(np.float32)


def _rmsnorm(x, g):
    ms = jnp.mean(x * x, axis=-1, keepdims=True)
    return x * lax.rsqrt(ms + EPS) * g


def _dot(a, b):
    return jnp.dot(a, b, preferred_element_type=F32)


def _dot_nt(a, b):
    return lax.dot_general(a, b, (((1,), (1,)), ((), ())), preferred_element_type=F32)


def _dot_tn(a, b):
    return lax.dot_general(a, b, (((0,), (0,)), ((), ())), preferred_element_type=F32)


def _gelu_tanh(x):
    c = np.float32(np.sqrt(2.0 / np.pi))
    return 0.5 * x * (1.0 + jnp.tanh(c * (x + 0.044715 * (x * x * x))))


def _sigmoid(x):
    return 0.5 * jnp.tanh(0.5 * x) + 0.5


def _silu(x):
    hx = 0.5 * x
    return hx * jnp.tanh(hx) + hx


def _lower_bound(lbl):
    mx = jnp.maximum(lbl[0:1, :], lbl[1:2, :])
    e0 = jnp.exp(lbl[0:1, :] - mx)
    e1 = jnp.exp(lbl[1:2, :] - mx)
    return e0 / (e0 + e1)


def _lanes(c):
    return slice(c * D_HEAD, (c + 1) * D_HEAD)


def _seg_order(n, rev):
    return range(n - 1, -1, -1) if rev else range(n)


def _scan_linear(a_ref, u_ref, carry, rev):
    hs = [jnp.zeros((SUBLANES, D_HEAD), F32)] * N_HEADS
    ps = [jnp.ones((SUBLANES, D_HEAD), F32)] * N_HEADS
    for gi in _seg_order(SEG, rev):
        idx = pl.ds(gi, SUBLANES, stride=SEG)
        for c in range(N_HEADS):
            a = a_ref[c, idx, :]
            hs[c] = a * hs[c] + u_ref[c, idx, :]
            ps[c] = a * ps[c]
            u_ref[c, idx, :] = hs[c]
            a_ref[c, idx, :] = ps[c]
    cs = [carry[:, _lanes(c)] for c in range(N_HEADS)]
    for r in _seg_order(SUBLANES, rev):
        rows = pl.ds(r * SEG, SEG)
        for c in range(N_HEADS):
            u_ref[c, rows, :] = u_ref[c, rows, :] + a_ref[c, rows, :] * cs[c]
            cs[c] = ps[c][r:r + 1, :] * cs[c] + hs[c][r:r + 1, :]
    return jnp.concatenate(cs, axis=1)


def _scan_sum(g_ref, rev):
    hs = [jnp.zeros((SUBLANES, D_HEAD), F32)] * N_HEADS
    for gi in _seg_order(SEG, rev):
        idx = pl.ds(gi, SUBLANES, stride=SEG)
        for c in range(N_HEADS):
            hs[c] = hs[c] + g_ref[c, idx, :]
            g_ref[c, idx, :] = hs[c]
    cs = [jnp.zeros((1, D_HEAD), F32)] * N_HEADS
    for r in _seg_order(SUBLANES, rev):
        rows = pl.ds(r * SEG, SEG)
        for c in range(N_HEADS):
            g_ref[c, rows, :] = g_ref[c, rows, :] + cs[c]
            cs[c] = cs[c] + hs[c][r:r + 1, :]


def _rglru_coef(lam_ref):
    nl = -lam_ref[...]
    return -RG_C * (jnp.maximum(nl, 0.0) + jnp.log1p(jnp.exp(-jnp.abs(nl))))


def _rglru_block(xc, c, rgw_ref, rgb_ref, coef):
    pre = _dot(xc.astype(BF16), rgw_ref[c])
    r = _sigmoid(pre[:, :D_HEAD] + rgb_ref[0:1, _lanes(c)])
    i = _sigmoid(pre[:, D_HEAD:] + rgb_ref[1:2, _lanes(c)])
    log_a = coef[:, _lanes(c)] * r
    a = jnp.exp(log_a)
    th = jnp.tanh(log_a)
    sq = -2.0 * th / (1.0 - th)
    mult = jnp.where(sq > 0.0, sq * lax.rsqrt(sq), 0.0)
    return a, mult * (i * xc)


def _hgrn2_tile(q_ref, k_ref, g_ref, b_ref, v_ref, p_ref, x_ref, qe_ref, ke_ref, st_ref, mask_ref, o_ref, rev,
                side_work=()):
    n = TILE
    heads = range(N_HEADS)
    edge = 0 if rev else n - 1
    sub = lax.broadcasted_iota(jnp.int32, (n, D_HEAD), 0) % SUBLANES

    def boundary(hd, m):
        parts = []
        for c in range(n // (2 * m)):
            r0 = c * 2 * m + (m if rev else m - 1)
            parts.append(jnp.broadcast_to(b_ref[hd, pl.ds(r0, 1), :], (2 * m, D_HEAD)))
        return jnp.concatenate(parts, axis=0) if len(parts) > 1 else parts[0]

    _scan_sum(b_ref, rev)

    for hd in heads:
        qh = q_ref[:,_lanes(hd)]
        kh = k_ref[hd]
        b = b_ref[hd]
        b_edge = b_ref[hd, pl.ds(edge, 1), :]
        x_ref[2, hd] = qh.astype(BF16)
        x_ref[3, hd] = kh.astype(BF16)
        qe_ref[hd] = (qh * jnp.exp2(b)).astype(BF16)
        ke_ref[hd] = (kh * jnp.exp2(b_edge - b)).astype(BF16)
    for hd in heads:
        p_ref[hd] = mask_ref[0] * _dot_nt(x_ref[2, hd], x_ref[3, hd])

    for li, m in enumerate(LEVELS):
        if li < len(side_work):
            side_work[li]()
        buf = li % 2
        is_q_blk = lambda jb: (jb % 2 == 1) != rev
        for hd in heads:
            qh = q_ref[:,_lanes(hd)]
            kh = k_ref[hd]
            if m < SUBLANES:
                pos = sub % (2 * m)
                is_q = (pos < m) if rev else (pos >= m)
                if m == 1:
                    e = jnp.where(is_q, g_ref[hd], 0.0)
                elif m == 2:
                    gh = g_ref[hd]
                    g_nxt = pltpu.roll(gh, n - 1, axis=0)
                    g_prv = pltpu.roll(gh, 1, axis=0)
                    if rev:
                        e = jnp.where(pos == 0, gh + g_nxt,
                                      jnp.where(pos == 1, gh, jnp.where(pos == 2, 0.0, g_prv)))
                    else:
                        e = jnp.where(pos == 0, g_nxt,
                                      jnp.where(pos == 1, 0.0, jnp.where(pos == 2, gh, gh + g_prv)))
                else:
                    e = -jnp.abs(b_ref[hd] - boundary(hd, m))
                sel = jnp.where(is_q, qh, kh)
            else:
                e = -jnp.abs(b_ref[hd] - boundary(hd, m))
                sel = jnp.concatenate(
                    [(qh if is_q_blk(jb) else kh)[jb * m:(jb + 1) * m] for jb in range(n // m)], axis=0)
            x_ref[buf, hd] = (sel * jnp.exp2(e)).astype(BF16)
        for hd in heads:
            if m < SUBLANES:
                xl = x_ref[buf, hd]
                p_ref[hd] = p_ref[hd] + mask_ref[1 + li] * _dot_nt(xl, xl)
                continue
            groups = range(n // (2 * m))
            q_rows = [slice(c * 2 * m + (0 if rev else m), c * 2 * m + (m if rev else 2 * m)) for c in groups]
            k_cols = [slice(c * 2 * m + (m if rev else 0), c * 2 * m + (2 * m if rev else m)) for c in groups]
            if m >= BF16_ROWS:
                lhs = jnp.concatenate([x_ref[buf, hd, rs, :] for rs in q_rows], axis=0) if len(q_rows) > 1 \
                    else x_ref[buf, hd, q_rows[0], :]
                s = _dot_nt(lhs, x_ref[buf, hd])
                s_rows = [slice(c * m, (c + 1) * m) for c in groups]
            else:
                xl = x_ref[buf, hd]
                s = _dot_nt(xl, xl)
                s_rows = q_rows
            for rs, sr, kc in zip(q_rows, s_rows, k_cols):
                p_ref[hd, rs, kc] = s[sr, kc]

    for hd in heads:
        st = st_ref[_lanes(hd), :]
        vh = v_ref[:,_lanes(hd)]
        b_edge = b_ref[hd, pl.ds(edge, 1), :]
        o_ref[:, _lanes(hd)] = (_dot(p_ref[hd].astype(BF16), vh)
                                + _dot_nt(qe_ref[hd], st.astype(BF16)))
        st_ref[_lanes(hd), :] = st * jnp.exp2(b_edge) + _dot_tn(vh, ke_ref[hd])


def _hgrn2_tile_direct(q_ref, k_ref, g_ref, b_ref, v_ref, p_ref, st_ref, mask_ref, o_ref, rev, side_work=()):
    n = TILE
    heads = range(N_HEADS)
    edge = 0 if rev else n - 1
    sub = lax.broadcasted_iota(jnp.int32, (n, D_HEAD), 0) % SUBLANES

    def boundary(hd, m):
        parts = []
        for c in range(n // (2 * m)):
            r0 = c * 2 * m + (m if rev else m - 1)
            parts.append(jnp.broadcast_to(b_ref[hd, pl.ds(r0, 1), :], (2 * m, D_HEAD)))
        return jnp.concatenate(parts, axis=0) if len(parts) > 1 else parts[0]

    _scan_sum(b_ref, rev)

    for hd in heads:
        qh = q_ref[:, _lanes(hd)]
        kh = k_ref[hd]
        gh = g_ref[hd]
        acc = mask_ref[0] * _dot_nt(qh.astype(BF16), kh.astype(BF16))
        for li, m in enumerate(FINE_LEVELS):
            pos = sub % (2 * m)
            is_q = (pos < m) if rev else (pos >= m)
            if m == 1:
                e = jnp.where(is_q, gh, 0.0)
            elif m == 2:
                g_nxt = pltpu.roll(gh, n - 1, axis=0)
                g_prv = pltpu.roll(gh, 1, axis=0)
                if rev:
                    e = jnp.where(pos == 0, gh + g_nxt,
                                  jnp.where(pos == 1, gh, jnp.where(pos == 2, 0.0, g_prv)))
                else:
                    e = jnp.where(pos == 0, g_nxt,
                                  jnp.where(pos == 1, 0.0, jnp.where(pos == 2, gh, gh + g_prv)))
            else:
                e = -jnp.abs(b_ref[hd] - boundary(hd, m))
            xl = (jnp.where(is_q, qh, kh) * jnp.exp2(e)).astype(BF16)
            acc = acc + mask_ref[1 + li] * _dot_nt(xl, xl)
        p_ref[hd] = acc

    coarse = [m for m in LEVELS if m >= SUBLANES]
    for ci, m in enumerate(coarse):
        if ci < len(side_work):
            side_work[ci]()
        is_q_blk = lambda jb: (jb % 2 == 1) != rev
        groups = range(n // (2 * m))
        q_rows = [slice(c * 2 * m + (0 if rev else m), c * 2 * m + (m if rev else 2 * m)) for c in groups]
        k_cols = [slice(c * 2 * m + (m if rev else 0), c * 2 * m + (2 * m if rev else m)) for c in groups]
        for hd in heads:
            qh = q_ref[:, _lanes(hd)]
            kh = k_ref[hd]
            sel = jnp.concatenate(
                [(qh if is_q_blk(jb) else kh)[jb * m:(jb + 1) * m] for jb in range(n // m)], axis=0)
            xl = (sel * jnp.exp2(-jnp.abs(b_ref[hd] - boundary(hd, m)))).astype(BF16)
            if m >= BF16_ROWS:
                lhs = jnp.concatenate([xl[rs] for rs in q_rows], axis=0) if len(q_rows) > 1 else xl[q_rows[0]]
                s = _dot_nt(lhs, xl)
                s_rows = [slice(c * m, (c + 1) * m) for c in groups]
            else:
                s = _dot_nt(xl, xl)
                s_rows = q_rows
            for rs, sr, kc in zip(q_rows, s_rows, k_cols):
                p_ref[hd, rs, kc] = s[sr, kc]

    for hd in heads:
        qh = q_ref[:, _lanes(hd)]
        b = b_ref[hd]
        b_edge = b_ref[hd, pl.ds(edge, 1), :]
        qe = (qh * jnp.exp2(b)).astype(BF16)
        ke = (k_ref[hd] * jnp.exp2(b_edge - b)).astype(BF16)
        st = st_ref[_lanes(hd), :]
        vh = v_ref[:, _lanes(hd)]
        o_ref[:, _lanes(hd)] = _dot(p_ref[hd].astype(BF16), vh) + _dot_nt(qe, st.astype(BF16))
        st_ref[_lanes(hd), :] = st * jnp.exp2(b_edge) + _dot_tn(vh, ke)


def _hgrn2_scratch():
    blk = (N_HEADS, TILE, D_HEAD)
    return [
        pltpu.VMEM(blk, F32),
        pltpu.VMEM(blk, F32),
        pltpu.VMEM(blk, F32),
        pltpu.VMEM((N_HEADS, TILE, TILE), F32),
        pltpu.VMEM((4,) + blk, BF16),
        pltpu.VMEM(blk, BF16),
        pltpu.VMEM(blk, BF16),
    ]


def _forget_gate_block(ff, lb, c, k_ref, g_ref, b_ref):
    f = lb[:, _lanes(c)] + (1.0 - lb[:, _lanes(c)]) * _sigmoid(ff)
    g = jnp.log(f) * LOG2_E
    k_ref[c] = 1.0 - f
    g_ref[c] = g
    b_ref[c] = g


def _fwd_kernel(x_ref, xn_ref, meta_ref, gmix_ref, w_ref, convw_ref, convb_ref, rgw_ref, rgb_ref, lam_ref,
                lbl_ref, mask_ref,
                xc_ref, q_ref, v_ref, hf_ref, of_ref,
                st_scr, hc_scr, xa_scr, a_scr, u_scr, z_scr, k_scr, g_scr, b_scr, p_scr, x_scr, qe_scr, ke_scr,
                *, natural_conv, direct):
    j = pl.program_id(1)
    n_real = pl.num_programs(1) - 1
    is_meta = j == 0

    @pl.when(is_meta)
    def _():
        st_scr[...] = jnp.zeros_like(st_scr)
        hc_scr[...] = jnp.zeros_like(hc_scr)
        xa_scr[:, 0:HALO, :] = jnp.zeros((N_HEADS, HALO, D_HEAD), F32)

    x_main = jnp.where(is_meta, meta_ref[...], x_ref[...])
    x_next = jnp.where(j == n_real, 0.0, xn_ref[...])
    hx = _rmsnorm(jnp.concatenate([x_main, x_next], axis=0), gmix_ref[...]).astype(BF16)
    hx_main = hx[0:TILE, :]

    for c in range(N_HEADS):
        ls = _lanes(c)
        if c % 2 == 0:
            xa = _dot(hx, w_ref[:, c * D_HEAD:(c + 2) * D_HEAD])
        xa_scr[c, HALO:, :] = xa[:, (c % 2) * D_HEAD:(c % 2 + 1) * D_HEAD]
        if natural_conv:
            acc = convb_ref[:, ls] + convw_ref[0:1, ls] * xa_scr[c, pl.ds(HALO - 2, TILE), :]
            for tap in range(1, 4):
                acc = acc + convw_ref[tap:tap + 1, ls] * xa_scr[c, pl.ds(HALO - 2 + tap, TILE), :]
            a_scr[c] = acc
        else:
            shifted = [xa_scr[c, pl.ds(HALO - 2 + i, SUBLANES, stride=SEG), :] for i in range(SEG + 3)]
            for gi in range(SEG):
                acc = convb_ref[:, ls] + convw_ref[0:1, ls] * shifted[gi]
                for tap in range(1, 4):
                    acc = acc + convw_ref[tap:tap + 1, ls] * shifted[gi + tap]
                a_scr[c, pl.ds(gi, SUBLANES, stride=SEG), :] = acc
        xa_scr[c, 0:HALO, :] = xa_scr[c, TILE:TILE + HALO, :]

    coef = _rglru_coef(lam_ref)
    row = lax.broadcasted_iota(jnp.int32, (TILE, D_HEAD), 0)
    pad = jnp.logical_and(is_meta, row < TILE - N_META)
    for c in range(N_HEADS):
        xc = a_scr[c]
        xc_ref[:, _lanes(c)] = xc
        a, u = _rglru_block(xc, c, rgw_ref, rgb_ref, coef)
        a_scr[c] = a
        u_scr[c] = jnp.where(pad, 0.0, u)
        if c % 2 == 1:
            i = c // 2
            for part in range(3):
                lo = part * D_MODEL + i * Z_CHUNK
                z_scr[:, lo:lo + Z_CHUNK] = _dot(hx_main, w_ref[:, D_MODEL + lo:D_MODEL + lo + Z_CHUNK])
    hc_scr[0:1, :] = _scan_linear(a_scr, u_scr, hc_scr[0:1, :], rev=False)
    for c in range(N_HEADS):
        hf_ref[:, _lanes(c)] = u_scr[c]

    lb = _lower_bound(lbl_ref[...])
    for c in range(N_HEADS):
        q_ref[:, _lanes(c)] = _silu(z_scr[:, _lanes(c)])
        _forget_gate_block(z_scr[:, D_MODEL + c * D_HEAD:D_MODEL + (c + 1) * D_HEAD], lb, c, k_scr, g_scr, b_scr)
        v_ref[:, _lanes(c)] = z_scr[:, 2 * D_MODEL + c * D_HEAD:2 * D_MODEL + (c + 1) * D_HEAD].astype(BF16)

    if direct:
        _hgrn2_tile_direct(q_ref, k_scr, g_scr, b_scr, v_ref, p_scr, st_scr, mask_ref, of_ref, rev=False)
    else:
        _hgrn2_tile(q_ref, k_scr, g_scr, b_scr, v_ref, p_scr, x_scr, qe_scr, ke_scr, st_scr, mask_ref, of_ref,
                    rev=False)


def _forward_sweep(x, meta_tile, p, natural_conv, direct):
    bsz, t, _ = x.shape
    nt = t // TILE
    halo_blocks = t // HALO
    row_blk = lambda b, j: (b, jnp.maximum(j - 1, 0), 0)
    const2 = lambda b, j: (0, 0)
    const3 = lambda b, j: (0, 0, 0)
    tok = pl.BlockSpec((None, TILE, D_MODEL), row_blk)
    blk = (N_HEADS, TILE, D_HEAD)
    out_f32 = jax.ShapeDtypeStruct((bsz, t, D_MODEL), F32)
    return pl.pallas_call(
        functools.partial(_fwd_kernel, natural_conv=natural_conv, direct=direct),
        grid=(bsz, nt + 1),
        in_specs=[
            tok,
            pl.BlockSpec((None, HALO, D_MODEL),
                         lambda b, j: (b, jnp.minimum(j * (TILE // HALO), halo_blocks - 1), 0)),
            pl.BlockSpec((TILE, D_MODEL), const2),
            pl.BlockSpec((1, D_MODEL), const2),
            pl.BlockSpec((D_MODEL, 4 * D_MODEL), const2),
            pl.BlockSpec((4, D_MODEL), const2),
            pl.BlockSpec((1, D_MODEL), const2),
            pl.BlockSpec((N_HEADS, D_HEAD, 2 * D_HEAD), const3),
            pl.BlockSpec((2, D_MODEL), const2),
            pl.BlockSpec((1, D_MODEL), const2),
            pl.BlockSpec((2, D_MODEL), const2),
            pl.BlockSpec((1 + len(FINE_LEVELS), TILE, TILE), const3),
        ],
        out_specs=[tok, tok, tok, tok, tok],
        out_shape=[out_f32, out_f32, jax.ShapeDtypeStruct((bsz, t, D_MODEL), BF16), out_f32, out_f32],
        scratch_shapes=[
            pltpu.VMEM((D_MODEL, D_HEAD), F32),
            pltpu.VMEM((SUBLANES, D_MODEL), F32),
            pltpu.VMEM((N_HEADS, TILE + 2 * HALO, D_HEAD), F32),
            pltpu.VMEM(blk, F32),
            pltpu.VMEM(blk, F32),
            pltpu.VMEM((TILE, 3 * D_MODEL), F32),
        ] + _hgrn2_scratch(),
        compiler_params=pltpu.CompilerParams(
            dimension_semantics=("arbitrary", "arbitrary"), vmem_limit_bytes=VMEM_LIMIT),
        name="fwd_sweep",
    )(x, x, meta_tile, p["gmix"], p["w_fwd"], p["conv_w"], p["conv_b"], p["rgw_f"], p["rgb_f"], p["lam_f"],
      p["lbl_f"], p["mask_f"])


def _bwd_kernel(x_ref, xc_ref, q_ref, v_ref, hf_ref, of_ref, gmix_ref, w_ref, rgw_ref, rgb_ref, lam_ref,
                lbl_ref, mask_ref, hgg_ref, wa_ref, wb_ref, wo_ref,
                y_ref,
                st_scr, hc_scr, z_scr, a_scr, u_scr, ob_scr, ma_scr, ba_scr, bb_scr,
                k_scr, g_scr, b_scr, p_scr, x_scr, qe_scr, ke_scr, *, direct):
    j = pl.program_id(1)

    @pl.when(j == 0)
    def _():
        st_scr[...] = jnp.zeros_like(st_scr)
        hc_scr[...] = jnp.zeros_like(hc_scr)

    hx = _rmsnorm(x_ref[...], gmix_ref[...]).astype(BF16)

    coef = _rglru_coef(lam_ref)
    for c in range(N_HEADS):
        a, u = _rglru_block(xc_ref[:, _lanes(c)], c, rgw_ref, rgb_ref, coef)
        a_scr[c] = a
        u_scr[c] = u
        if c % 2 == 1:
            i = c // 2
            for part in range(5):
                lo = part * D_MODEL + i * Z_CHUNK
                z_scr[:, lo:lo + Z_CHUNK] = _dot(hx, w_ref[:, lo:lo + Z_CHUNK])
    hc_scr[0:1, :] = _scan_linear(a_scr, u_scr, hc_scr[0:1, :], rev=True)

    lb = _lower_bound(lbl_ref[...])
    for c in range(N_HEADS):
        ls = _lanes(c)
        _forget_gate_block(z_scr[:, D_MODEL + c * D_HEAD:D_MODEL + (c + 1) * D_HEAD], lb, c, k_scr, g_scr, b_scr)
        ba_scr[:, ls] = ((hf_ref[:, ls] + u_scr[c]) * _gelu_tanh(z_scr[:, ls])).astype(BF16)

    def branch_a_slice(i):
        def run():
            cols = slice(i * Z_CHUNK, (i + 1) * Z_CHUNK)
            gate = _sigmoid(z_scr[:, 3 * D_MODEL + i * Z_CHUNK:3 * D_MODEL + (i + 1) * Z_CHUNK])
            ma_scr[:, cols] = gate * _dot(ba_scr[...], wa_ref[:, cols])
        return run

    side_work = [branch_a_slice(i) for i in range(D_MODEL // Z_CHUNK)]
    if direct:
        _hgrn2_tile_direct(q_ref, k_scr, g_scr, b_scr, v_ref, p_scr, st_scr, mask_ref, ob_scr, rev=True,
                           side_work=side_work)
    else:
        _hgrn2_tile(q_ref, k_scr, g_scr, b_scr, v_ref, p_scr, x_scr, qe_scr, ke_scr, st_scr, mask_ref, ob_scr,
                    rev=True, side_work=side_work)

    for c in range(N_HEADS):
        ls = _lanes(c)
        o = of_ref[:, ls] + ob_scr[:, ls]
        o = o * lax.rsqrt(jnp.mean(o * o, axis=-1, keepdims=True) + EPS)
        og = z_scr[:, 2 * D_MODEL + c * D_HEAD:2 * D_MODEL + (c + 1) * D_HEAD]
        bb_scr[:, ls] = (o * hgg_ref[:, ls] * _silu(og)).astype(BF16)

    merged = ma_scr[...] + _sigmoid(z_scr[:, 4 * D_MODEL:5 * D_MODEL]) * _dot(bb_scr[...], wb_ref[...])
    y_ref[...] = x_ref[...] + _dot(merged.astype(BF16), wo_ref[...])


def _backward_sweep(x, xc, q, v, hf, of, p, direct):
    bsz, t, _ = x.shape
    nt = t // TILE
    row_blk = lambda b, j: (b, nt - 1 - j, 0)
    const2 = lambda b, j: (0, 0)
    const3 = lambda b, j: (0, 0, 0)
    tok = pl.BlockSpec((None, TILE, D_MODEL), row_blk)
    sq = pl.BlockSpec((D_MODEL, D_MODEL), const2)
    vec = pl.BlockSpec((1, D_MODEL), const2)
    blk = (N_HEADS, TILE, D_HEAD)
    return pl.pallas_call(
        functools.partial(_bwd_kernel, direct=direct),
        grid=(bsz, nt),
        in_specs=[
            tok, tok, tok, tok, tok, tok,
            vec,
            pl.BlockSpec((D_MODEL, 5 * D_MODEL), const2),
            pl.BlockSpec((N_HEADS, D_HEAD, 2 * D_HEAD), const3),
            pl.BlockSpec((2, D_MODEL), const2),
            vec,
            pl.BlockSpec((2, D_MODEL), const2),
            pl.BlockSpec((1 + len(FINE_LEVELS), TILE, TILE), const3),
            vec, sq, sq, sq,
        ],
        out_specs=tok,
        out_shape=jax.ShapeDtypeStruct((bsz, t, D_MODEL), F32),
        scratch_shapes=[
            pltpu.VMEM((D_MODEL, D_HEAD), F32),
            pltpu.VMEM((SUBLANES, D_MODEL), F32),
            pltpu.VMEM((TILE, 5 * D_MODEL), F32),
            pltpu.VMEM(blk, F32),
            pltpu.VMEM(blk, F32),
            pltpu.VMEM((TILE, D_MODEL), F32),
            pltpu.VMEM((TILE, D_MODEL), F32),
            pltpu.VMEM((TILE, D_MODEL), BF16),
            pltpu.VMEM((TILE, D_MODEL), BF16),
        ] + _hgrn2_scratch(),
        compiler_params=pltpu.CompilerParams(
            dimension_semantics=("arbitrary", "arbitrary"), vmem_limit_bytes=VMEM_LIMIT),
        name="bwd_sweep",
    )(x, xc, q, v, hf, of, p["gmix"], p["w_bwd"], p["rgw_b"], p["rgb_b"], p["lam_b"], p["lbl_b"],
      p["mask_b"], p["hgg"], p["wa"], p["wb"], p["wo"])


def _mlp_kernel(x_ref, gmlp_ref, w1_ref, w2_ref, gfin_ref, y_ref):
    x = x_ref[...]
    hx = _rmsnorm(x, gmlp_ref[...]).astype(BF16)
    acc = x
    for c in range(D_FF // FF_CHUNK):
        cols = slice(c * FF_CHUNK, (c + 1) * FF_CHUNK)
        hm = jnp.maximum(_dot(hx, w1_ref[:, cols]), 0.0)
        acc = acc + _dot((hm * hm).astype(BF16), w2_ref[cols, :])
    y_ref[...] = _rmsnorm(acc, gfin_ref[...])


def _channel_mixer(x, p):
    bsz, t, _ = x.shape
    rows = bsz * t
    x2 = x.reshape(rows, D_MODEL)
    tok = pl.BlockSpec((MLP_TILE, D_MODEL), lambda i: (i, 0))
    const2 = lambda i: (0, 0)
    vec = pl.BlockSpec((1, D_MODEL), const2)
    y = pl.pallas_call(
        _mlp_kernel,
        grid=(rows // MLP_TILE,),
        in_specs=[tok, vec, pl.BlockSpec((D_MODEL, D_FF), const2), pl.BlockSpec((D_FF, D_MODEL), const2), vec],
        out_specs=tok,
        out_shape=jax.ShapeDtypeStruct((rows, D_MODEL), F32),
        compiler_params=pltpu.CompilerParams(
            dimension_semantics=("arbitrary",), vmem_limit_bytes=VMEM_LIMIT),
        name="channel_mixer",
    )(x2, p["gmlp"], p["w_mlp1"], p["w_mlp2"], p["gfin"])
    return y.reshape(bsz, t, D_MODEL)


def _prepare_params(meta_tokens, hg_lb_logits, norm_mix_g, w_in, conv_w, conv_b, rg_wa, rg_ba, rg_wx, rg_bx,
                    rg_lambda, hg_norm_g, w_branch_a, w_branch_b, w_out, norm_mlp_g, w_mlp1, w_mlp2,
                    final_norm_g):
    d = D_MODEL
    w = w_in[0].astype(BF16)
    col = lambda i: w[:, i * d:(i + 1) * d]
    row = lambda a: a.reshape(1, d).astype(F32)
    rgw = lambda k: jnp.concatenate([rg_wa[0, k], rg_wx[0, k]], axis=-1).astype(BF16)
    rgb = lambda k: jnp.stack([rg_ba[0, k], rg_bx[0, k]]).astype(F32)
    meta_tile = jnp.concatenate(
        [jnp.zeros((TILE - N_META, d), F32), meta_tokens.astype(F32)], axis=0)
    p = dict(
        gmix=row(norm_mix_g[0]),
        w_fwd=jnp.concatenate([col(0), col(2), col(3), col(5)], axis=1),
        w_bwd=jnp.concatenate([col(1), col(4), col(6), col(7), col(8)], axis=1),
        conv_w=conv_w[0].astype(F32), conv_b=row(conv_b[0]),
        rgw_f=rgw(0), rgw_b=rgw(1), rgb_f=rgb(0), rgb_b=rgb(1),
        lam_f=row(rg_lambda[0, 0]), lam_b=row(rg_lambda[0, 1]),
        lbl_f=hg_lb_logits[:, 0, :].astype(F32), lbl_b=hg_lb_logits[:, 1, :].astype(F32),
        mask_f=jnp.asarray(_fine_masks(TILE, False)), mask_b=jnp.asarray(_fine_masks(TILE, True)),
        hgg=row(hg_norm_g[0]),
        wa=w_branch_a[0].astype(BF16), wb=w_branch_b[0].astype(BF16), wo=w_out[0].astype(BF16),
        gmlp=row(norm_mlp_g[0]), w_mlp1=w_mlp1[0].astype(BF16), w_mlp2=w_mlp2[0].astype(BF16),
        gfin=row(final_norm_g),
    )
    return meta_tile, p


def _encode(x, meta_tile, p, natural_conv, direct_fwd, direct_bwd):
    assert x.shape[1] % TILE == 0 and (x.shape[0] * x.shape[1]) % MLP_TILE == 0
    xc, q, v, hf, of = _forward_sweep(x, meta_tile, p, natural_conv, direct_fwd)
    x_mid = _backward_sweep(x, xc, q, v, hf, of, p, direct_bwd)
    return _channel_mixer(x_mid, p)


def kernel(x_prompt, x_sample, meta_tokens, hg_lb_logits, norm_mix_g, w_in, conv_w, conv_b, rg_wa, rg_ba, rg_wx,
           rg_bx, rg_lambda, hg_norm_g, w_branch_a, w_branch_b, w_out, norm_mlp_g, w_mlp1, w_mlp2, final_norm_g):
    meta_tile, p = _prepare_params(meta_tokens, hg_lb_logits, norm_mix_g, w_in, conv_w, conv_b, rg_wa, rg_ba,
                                   rg_wx, rg_bx, rg_lambda, hg_norm_g, w_branch_a, w_branch_b, w_out,
                                   norm_mlp_g, w_mlp1, w_mlp2, final_norm_g)
    return (_encode(x_prompt, meta_tile, p, natural_conv=True, direct_fwd=False, direct_bwd=False),
            _encode(x_sample, meta_tile, p, natural_conv=False, direct_fwd=True, direct_bwd=True))
```

```python
import functools

import numpy as np
import jax
import jax.numpy as jnp
from jax import lax
from jax.experimental import pallas as pl
from jax.experimental.pallas import tpu as pltpu

D_MODEL = 1024
N_META = 16
N_HEADS = 8
D_HEAD = D_MODEL // N_HEADS
D_FF = 4 * D_MODEL
RG_C = 8.0
EPS = 1e-6
LOG2_E = float(np.log2(np.e))

SUBLANES = 8
BF16_ROWS = 16
TILE = 128
SEG = TILE // SUBLANES
HALO = SUBLANES
MLP_TILE = 512
FF_CHUNK = 1024
Z_CHUNK = 256
VMEM_LIMIT = 56 * 1024 * 1024

F32 = jnp.float32
BF16 = jnp.bfloat16


def _level_half_sizes(n):
    out, m = [], 1
    while m < n:
        out.append(m)
        m *= 2
    return tuple(out)


LEVELS = _level_half_sizes(TILE)
FINE_LEVELS = tuple(m for m in LEVELS if m < SUBLANES)


def _fine_masks(n, rev):
    t = np.arange(n)[:, None]
    s = np.arange(n)[None, :]
    masks = [t == s]
    for m in FINE_LEVELS:
        same = (t // (2 * m)) == (s // (2 * m))
        t_up = (t % (2 * m)) >= m
        s_up = (s % (2 * m)) >= m
        masks.append(same & (~t_up) & s_up if rev else same & t_up & (~s_up))
    masks.append(s >= t if rev else s <= t)
    return np.stack(masks).astype(np.float32)


def _rmsnorm(x, g):
    ms = jnp.mean(x * x, axis=-1, keepdims=True)
    return x * lax.rsqrt(ms + EPS) * g


def _dot(a, b):
    return jnp.dot(a, b, preferred_element_type=F32)


def _dot_nt(a, b):
    return lax.dot_general(a, b, (((1,), (1,)), ((), ())), preferred_element_type=F32)


def _dot_tn(a, b):
    return lax.dot_general(a, b, (((0,), (0,)), ((), ())), preferred_element_type=F32)


def _gelu_tanh(x):
    c = np.float32(np.sqrt(2.0 / np.pi))
    return 0.5 * x * (1.0 + jnp.tanh(c * (x + 0.044715 * (x * x * x))))


def _sigmoid(x):
    return 0.5 * jnp.tanh(0.5 * x) + 0.5


def _silu(x):
    hx = 0.5 * x
    return hx * jnp.tanh(hx) + hx


def _lower_bound(lbl):
    mx = jnp.maximum(lbl[0:1, :], lbl[1:2, :])
    e0 = jnp.exp(lbl[0:1, :] - mx)
    e1 = jnp.exp(lbl[1:2, :] - mx)
    return e0 / (e0 + e1)


def _lanes(c):
    return slice(c * D_HEAD, (c + 1) * D_HEAD)


def _seg_order(n, rev):
    return range(n - 1, -1, -1) if rev else range(n)


def _scan_linear(a_ref, u_ref, carry, rev):
    hs = [jnp.zeros((SUBLANES, D_HEAD), F32)] * N_HEADS
    ps = [jnp.ones((SUBLANES, D_HEAD), F32)] * N_HEADS
    for gi in _seg_order(SEG, rev):
        idx = pl.ds(gi, SUBLANES, stride=SEG)
        for c in range(N_HEADS):
            a = a_ref[c, idx, :]
            hs[c] = a * hs[c] + u_ref[c, idx, :]
            ps[c] = a * ps[c]
            u_ref[c, idx, :] = hs[c]
            a_ref[c, idx, :] = ps[c]
    cs = [carry[:, _lanes(c)] for c in range(N_HEADS)]
    for r in _seg_order(SUBLANES, rev):
        rows = pl.ds(r * SEG, SEG)
        for c in range(N_HEADS):
            u_ref[c, rows, :] = u_ref[c, rows, :] + a_ref[c, rows, :] * cs[c]
            cs[c] = ps[c][r:r + 1, :] * cs[c] + hs[c][r:r + 1, :]
    return jnp.concatenate(cs, axis=1)


def _scan_linear_2pass(a_ref, u_ref, carry, rev):
    hs = [jnp.zeros((SUBLANES, D_HEAD), F32)] * N_HEADS
    ps = [jnp.ones((SUBLANES, D_HEAD), F32)] * N_HEADS
    for gi in _seg_order(SEG, rev):
        idx = pl.ds(gi, SUBLANES, stride=SEG)
        for c in range(N_HEADS):
            a = a_ref[c, idx, :]
            hs[c] = a * hs[c] + u_ref[c, idx, :]
            ps[c] = a * ps[c]
    cs = [carry[:, _lanes(c)] for c in range(N_HEADS)]
    starts = [[None] * SUBLANES for _ in range(N_HEADS)]
    for r in _seg_order(SUBLANES, rev):
        for c in range(N_HEADS):
            starts[c][r] = cs[c]
            cs[c] = ps[c][r:r + 1, :] * cs[c] + hs[c][r:r + 1, :]
    hs = [jnp.concatenate(starts[c], axis=0) for c in range(N_HEADS)]
    for gi in _seg_order(SEG, rev):
        idx = pl.ds(gi, SUBLANES, stride=SEG)
        for c in range(N_HEADS):
            hs[c] = a_ref[c, idx, :] * hs[c] + u_ref[c, idx, :]
            u_ref[c, idx, :] = hs[c]
    return jnp.concatenate(cs, axis=1)


def _cumsum_mxu(g, tri):
    hi = g.astype(BF16)
    r1 = g - hi.astype(F32)
    mid = r1.astype(BF16)
    lo = (r1 - mid.astype(F32)).astype(BF16)
    return _dot(tri, hi) + _dot(tri, mid) + _dot(tri, lo)


def _scan_sum(g_ref, rev):
    hs = [jnp.zeros((SUBLANES, D_HEAD), F32)] * N_HEADS
    for gi in _seg_order(SEG, rev):
        idx = pl.ds(gi, SUBLANES, stride=SEG)
        for c in range(N_HEADS):
            hs[c] = hs[c] + g_ref[c, idx, :]
            g_ref[c, idx, :] = hs[c]
    cs = [jnp.zeros((1, D_HEAD), F32)] * N_HEADS
    for r in _seg_order(SUBLANES, rev):
        rows = pl.ds(r * SEG, SEG)
        for c in range(N_HEADS):
            g_ref[c, rows, :] = g_ref[c, rows, :] + cs[c]
            cs[c] = cs[c] + hs[c][r:r + 1, :]


def _rglru_coef(lam_ref):
    nl = -lam_ref[...]
    return -RG_C * (jnp.maximum(nl, 0.0) + jnp.log1p(jnp.exp(-jnp.abs(nl))))


def _rglru_block(xc, c, rgw_ref, rgb_ref, coef):
    pre = _dot(xc.astype(BF16), rgw_ref[c])
    r = _sigmoid(pre[:, :D_HEAD] + rgb_ref[0:1, _lanes(c)])
    i = _sigmoid(pre[:, D_HEAD:] + rgb_ref[1:2, _lanes(c)])
    log_a = coef[:, _lanes(c)] * r
    a = jnp.exp(log_a)
    th = jnp.tanh(log_a)
    sq = -2.0 * th / (1.0 - th)
    mult = jnp.where(sq > 0.0, sq * lax.rsqrt(sq), 0.0)
    return a, mult * (i * xc)


def _hgrn2_tile(q_ref, k_ref, g_ref, b_ref, v_ref, p_ref, x_ref, qe_ref, ke_ref, st_ref, mask_ref, o_ref, rev,
                side_work=()):
    n = TILE
    heads = range(N_HEADS)
    edge = 0 if rev else n - 1
    sub = lax.broadcasted_iota(jnp.int32, (n, D_HEAD), 0) % SUBLANES

    def boundary(hd, m):
        parts = []
        for c in range(n // (2 * m)):
            r0 = c * 2 * m + (m if rev else m - 1)
            parts.append(jnp.broadcast_to(b_ref[hd, pl.ds(r0, 1), :], (2 * m, D_HEAD)))
        return jnp.concatenate(parts, axis=0) if len(parts) > 1 else parts[0]

    _scan_sum(b_ref, rev)

    for hd in heads:
        qh = q_ref[:,_lanes(hd)]
        kh = k_ref[hd]
        b = b_ref[hd]
        b_edge = b_ref[hd, pl.ds(edge, 1), :]
        x_ref[2, hd] = qh.astype(BF16)
        x_ref[3, hd] = kh.astype(BF16)
        qe_ref[hd] = (qh * jnp.exp2(b)).astype(BF16)
        ke_ref[hd] = (kh * jnp.exp2(b_edge - b)).astype(BF16)
    for hd in heads:
        p_ref[hd] = mask_ref[0] * _dot_nt(x_ref[2, hd], x_ref[3, hd])

    for li, m in enumerate(LEVELS):
        if li < len(side_work):
            side_work[li]()
        buf = li % 2
        is_q_blk = lambda jb: (jb % 2 == 1) != rev
        for hd in heads:
            qh = q_ref[:,_lanes(hd)]
            kh = k_ref[hd]
            if m < SUBLANES:
                pos = sub % (2 * m)
                is_q = (pos < m) if rev else (pos >= m)
                if m == 1:
                    e = jnp.where(is_q, g_ref[hd], 0.0)
                elif m == 2:
                    gh = g_ref[hd]
                    g_nxt = pltpu.roll(gh, n - 1, axis=0)
                    g_prv = pltpu.roll(gh, 1, axis=0)
                    if rev:
                        e = jnp.where(pos == 0, gh + g_nxt,
                                      jnp.where(pos == 1, gh, jnp.where(pos == 2, 0.0, g_prv)))
                    else:
                        e = jnp.where(pos == 0, g_nxt,
                                      jnp.where(pos == 1, 0.0, jnp.where(pos == 2, gh, gh + g_prv)))
                else:
                    e = -jnp.abs(b_ref[hd] - boundary(hd, m))
                sel = jnp.where(is_q, qh, kh)
            else:
                e = -jnp.abs(b_ref[hd] - boundary(hd, m))
                sel = jnp.concatenate(
                    [(qh if is_q_blk(jb) else kh)[jb * m:(jb + 1) * m] for jb in range(n // m)], axis=0)
            x_ref[buf, hd] = (sel * jnp.exp2(e)).astype(BF16)
        for hd in heads:
            if m < SUBLANES:
                xl = x_ref[buf, hd]
                p_ref[hd] = p_ref[hd] + mask_ref[1 + li] * _dot_nt(xl, xl)
                continue
            groups = range(n // (2 * m))
            q_rows = [slice(c * 2 * m + (0 if rev else m), c * 2 * m + (m if rev else 2 * m)) for c in groups]
            k_cols = [slice(c * 2 * m + (m if rev else 0), c * 2 * m + (2 * m if rev else m)) for c in groups]
            if m >= BF16_ROWS:
                lhs = jnp.concatenate([x_ref[buf, hd, rs, :] for rs in q_rows], axis=0) if len(q_rows) > 1 \
                    else x_ref[buf, hd, q_rows[0], :]
                s = _dot_nt(lhs, x_ref[buf, hd])
                s_rows = [slice(c * m, (c + 1) * m) for c in groups]
            else:
                xl = x_ref[buf, hd]
                s = _dot_nt(xl, xl)
                s_rows = q_rows
            for rs, sr, kc in zip(q_rows, s_rows, k_cols):
                p_ref[hd, rs, kc] = s[sr, kc]

    for hd in heads:
        st = st_ref[_lanes(hd), :]
        vh = v_ref[:,_lanes(hd)]
        b_edge = b_ref[hd, pl.ds(edge, 1), :]
        o_ref[:, _lanes(hd)] = (_dot(p_ref[hd].astype(BF16), vh)
                                + _dot_nt(qe_ref[hd], st.astype(BF16)))
        st_ref[_lanes(hd), :] = st * jnp.exp2(b_edge) + _dot_tn(vh, ke_ref[hd])


def _hgrn2_tile_direct(q_ref, k_ref, g_ref, b_ref, v_ref, p_ref, st_ref, mask_ref, o_ref, rev, side_work=(),
                       prescanned=False):
    n = TILE
    heads = range(N_HEADS)
    edge = 0 if rev else n - 1
    sub = lax.broadcasted_iota(jnp.int32, (n, D_HEAD), 0) % SUBLANES

    def boundary(hd, m):
        parts = []
        for c in range(n // (2 * m)):
            r0 = c * 2 * m + (m if rev else m - 1)
            parts.append(jnp.broadcast_to(b_ref[hd, pl.ds(r0, 1), :], (2 * m, D_HEAD)))
        return jnp.concatenate(parts, axis=0) if len(parts) > 1 else parts[0]

    if not prescanned:
        _scan_sum(b_ref, rev)

    for hd in heads:
        qh = q_ref[:, _lanes(hd)]
        kh = k_ref[hd]
        gh = g_ref[hd]
        acc = mask_ref[0] * _dot_nt(qh.astype(BF16), kh.astype(BF16))
        for li, m in enumerate(FINE_LEVELS):
            pos = sub % (2 * m)
            is_q = (pos < m) if rev else (pos >= m)
            if m == 1:
                e = jnp.where(is_q, gh, 0.0)
            elif m == 2:
                g_nxt = pltpu.roll(gh, n - 1, axis=0)
                g_prv = pltpu.roll(gh, 1, axis=0)
                if rev:
                    e = jnp.where(pos == 0, gh + g_nxt,
                                  jnp.where(pos == 1, gh, jnp.where(pos == 2, 0.0, g_prv)))
                else:
                    e = jnp.where(pos == 0, g_nxt,
                                  jnp.where(pos == 1, 0.0, jnp.where(pos == 2, gh, gh + g_prv)))
            else:
                e = -jnp.abs(b_ref[hd] - boundary(hd, m))
            xl = (jnp.where(is_q, qh, kh) * jnp.exp2(e)).astype(BF16)
            acc = acc + mask_ref[1 + li] * _dot_nt(xl, xl)
        p_ref[hd] = acc

    coarse = [m for m in LEVELS if m >= SUBLANES]
    for ci, m in enumerate(coarse):
        if ci < len(side_work):
            side_work[ci]()
        is_q_blk = lambda jb: (jb % 2 == 1) != rev
        groups = range(n // (2 * m))
        q_rows = [slice(c * 2 * m + (0 if rev else m), c * 2 * m + (m if rev else 2 * m)) for c in groups]
        k_cols = [slice(c * 2 * m + (m if rev else 0), c * 2 * m + (2 * m if rev else m)) for c in groups]
        for hd in heads:
            qh = q_ref[:, _lanes(hd)]
            kh = k_ref[hd]
            sel = jnp.concatenate(
                [(qh if is_q_blk(jb) else kh)[jb * m:(jb + 1) * m] for jb in range(n // m)], axis=0)
            xl = (sel * jnp.exp2(-jnp.abs(b_ref[hd] - boundary(hd, m)))).astype(BF16)
            if m >= BF16_ROWS:
                lhs = jnp.concatenate([xl[rs] for rs in q_rows], axis=0) if len(q_rows) > 1 else xl[q_rows[0]]
                s = _dot_nt(lhs, xl)
                s_rows = [slice(c * m, (c + 1) * m) for c in groups]
            else:
                s = _dot_nt(xl, xl)
                s_rows = q_rows
            for rs, sr, kc in zip(q_rows, s_rows, k_cols):
                p_ref[hd, rs, kc] = s[sr, kc]

    for hd in heads:
        qh = q_ref[:, _lanes(hd)]
        b = b_ref[hd]
        b_edge = b_ref[hd, pl.ds(edge, 1), :]
        qe = (qh * jnp.exp2(b)).astype(BF16)
        ke = (k_ref[hd] * jnp.exp2(b_edge - b)).astype(BF16)
        st = st_ref[_lanes(hd), :]
        vh = v_ref[:, _lanes(hd)]
        o_ref[:, _lanes(hd)] = _dot(p_ref[hd].astype(BF16), vh) + _dot_nt(qe, st.astype(BF16))
        st_ref[_lanes(hd), :] = st * jnp.exp2(b_edge) + _dot_tn(vh, ke)


def _hgrn2_scratch():
    blk = (N_HEADS, TILE, D_HEAD)
    return [
        pltpu.VMEM(blk, F32),
        pltpu.VMEM(blk, F32),
        pltpu.VMEM(blk, F32),
        pltpu.VMEM((N_HEADS, TILE, TILE), F32),
        pltpu.VMEM((4,) + blk, BF16),
        pltpu.VMEM(blk, BF16),
        pltpu.VMEM(blk, BF16),
    ]


def _forget_gate_block(ff, lb, c, k_ref, g_ref, b_ref, tri=None):
    f = lb[:, _lanes(c)] + (1.0 - lb[:, _lanes(c)]) * _sigmoid(ff)
    g = jnp.log(f) * LOG2_E
    k_ref[c] = 1.0 - f
    g_ref[c] = g
    b_ref[c] = g if tri is None else _cumsum_mxu(g, tri)


def _fwd_kernel(x_ref, xn_ref, meta_ref, gmix_ref, w_ref, convw_ref, convb_ref, rgw_ref, rgb_ref, lam_ref,
                lbl_ref, mask_ref,
                xc_ref, q_ref, v_ref, hf_ref, of_ref,
                st_scr, hc_scr, xa_scr, a_scr, u_scr, z_scr, k_scr, g_scr, b_scr, p_scr, x_scr, qe_scr, ke_scr,
                *, mxu_cumsum, two_pass):
    natural_conv = direct = True
    scan = _scan_linear_2pass if two_pass else _scan_linear
    tri = mask_ref[1 + len(FINE_LEVELS)].astype(BF16) if mxu_cumsum else None
    j = pl.program_id(1)
    n_real = pl.num_programs(1) - 1
    is_meta = j == 0

    @pl.when(is_meta)
    def _():
        st_scr[...] = jnp.zeros_like(st_scr)
        hc_scr[...] = jnp.zeros_like(hc_scr)
        xa_scr[:, 0:HALO, :] = jnp.zeros((N_HEADS, HALO, D_HEAD), F32)

    x_main = jnp.where(is_meta, meta_ref[...], x_ref[...])
    x_next = jnp.where(j == n_real, 0.0, xn_ref[...])
    hx = _rmsnorm(jnp.concatenate([x_main, x_next], axis=0), gmix_ref[...]).astype(BF16)
    hx_main = hx[0:TILE, :]

    for c in range(N_HEADS):
        ls = _lanes(c)
        if c % 2 == 0:
            xa = _dot(hx, w_ref[:, c * D_HEAD:(c + 2) * D_HEAD])
        xa_scr[c, HALO:, :] = xa[:, (c % 2) * D_HEAD:(c % 2 + 1) * D_HEAD]
        if natural_conv:
            acc = convb_ref[:, ls] + convw_ref[0:1, ls] * xa_scr[c, pl.ds(HALO - 2, TILE), :]
            for tap in range(1, 4):
                acc = acc + convw_ref[tap:tap + 1, ls] * xa_scr[c, pl.ds(HALO - 2 + tap, TILE), :]
            a_scr[c] = acc
        else:
            shifted = [xa_scr[c, pl.ds(HALO - 2 + i, SUBLANES, stride=SEG), :] for i in range(SEG + 3)]
            for gi in range(SEG):
                acc = convb_ref[:, ls] + convw_ref[0:1, ls] * shifted[gi]
                for tap in range(1, 4):
                    acc = acc + convw_ref[tap:tap + 1, ls] * shifted[gi + tap]
                a_scr[c, pl.ds(gi, SUBLANES, stride=SEG), :] = acc
        xa_scr[c, 0:HALO, :] = xa_scr[c, TILE:TILE + HALO, :]

    coef = _rglru_coef(lam_ref)
    row = lax.broadcasted_iota(jnp.int32, (TILE, D_HEAD), 0)
    pad = jnp.logical_and(is_meta, row < TILE - N_META)
    for c in range(N_HEADS):
        xc = a_scr[c]
        xc_ref[:, _lanes(c)] = xc
        a, u = _rglru_block(xc, c, rgw_ref, rgb_ref, coef)
        a_scr[c] = a
        u_scr[c] = jnp.where(pad, 0.0, u)
        if c % 2 == 1:
            i = c // 2
            for part in range(3):
                lo = part * D_MODEL + i * Z_CHUNK
                z_scr[:, lo:lo + Z_CHUNK] = _dot(hx_main, w_ref[:, D_MODEL + lo:D_MODEL + lo + Z_CHUNK])
    hc_scr[0:1, :] = scan(a_scr, u_scr, hc_scr[0:1, :], rev=False)
    for c in range(N_HEADS):
        hf_ref[:, _lanes(c)] = u_scr[c]

    lb = _lower_bound(lbl_ref[...])
    for c in range(N_HEADS):
        q_ref[:, _lanes(c)] = _silu(z_scr[:, _lanes(c)])
        _forget_gate_block(z_scr[:, D_MODEL + c * D_HEAD:D_MODEL + (c + 1) * D_HEAD], lb, c, k_scr, g_scr, b_scr, tri)
        v_ref[:, _lanes(c)] = z_scr[:, 2 * D_MODEL + c * D_HEAD:2 * D_MODEL + (c + 1) * D_HEAD].astype(BF16)

    if direct:
        _hgrn2_tile_direct(q_ref, k_scr, g_scr, b_scr, v_ref, p_scr, st_scr, mask_ref, of_ref, rev=False,
                           prescanned=mxu_cumsum)
    else:
        _hgrn2_tile(q_ref, k_scr, g_scr, b_scr, v_ref, p_scr, x_scr, qe_scr, ke_scr, st_scr, mask_ref, of_ref,
                    rev=False)


def _forward_sweep(x, meta_tile, p, mxu_cumsum, two_pass):
    bsz, t, _ = x.shape
    nt = t // TILE
    halo_blocks = t // HALO
    row_blk = lambda b, j: (b, jnp.maximum(j - 1, 0), 0)
    const2 = lambda b, j: (0, 0)
    const3 = lambda b, j: (0, 0, 0)
    tok = pl.BlockSpec((None, TILE, D_MODEL), row_blk)
    blk = (N_HEADS, TILE, D_HEAD)
    out_f32 = jax.ShapeDtypeStruct((bsz, t, D_MODEL), F32)
    return pl.pallas_call(
        functools.partial(_fwd_kernel, mxu_cumsum=mxu_cumsum, two_pass=two_pass),
        grid=(bsz, nt + 1),
        in_specs=[
            tok,
            pl.BlockSpec((None, HALO, D_MODEL),
                         lambda b, j: (b, jnp.minimum(j * (TILE // HALO), halo_blocks - 1), 0)),
            pl.BlockSpec((TILE, D_MODEL), const2),
            pl.BlockSpec((1, D_MODEL), const2),
            pl.BlockSpec((D_MODEL, 4 * D_MODEL), const2),
            pl.BlockSpec((4, D_MODEL), const2),
            pl.BlockSpec((1, D_MODEL), const2),
            pl.BlockSpec((N_HEADS, D_HEAD, 2 * D_HEAD), const3),
            pl.BlockSpec((2, D_MODEL), const2),
            pl.BlockSpec((1, D_MODEL), const2),
            pl.BlockSpec((2, D_MODEL), const2),
            pl.BlockSpec((2 + len(FINE_LEVELS), TILE, TILE), const3),
        ],
        out_specs=[tok, tok, tok, tok, tok],
        out_shape=[out_f32, out_f32, jax.ShapeDtypeStruct((bsz, t, D_MODEL), BF16), out_f32, out_f32],
        scratch_shapes=[
            pltpu.VMEM((D_MODEL, D_HEAD), F32),
            pltpu.VMEM((SUBLANES, D_MODEL), F32),
            pltpu.VMEM((N_HEADS, TILE + 2 * HALO, D_HEAD), F32),
            pltpu.VMEM(blk, F32),
            pltpu.VMEM(blk, F32),
            pltpu.VMEM((TILE, 3 * D_MODEL), F32),
        ] + _hgrn2_scratch(),
        compiler_params=pltpu.CompilerParams(
            dimension_semantics=("arbitrary", "arbitrary"), vmem_limit_bytes=VMEM_LIMIT),
        name="fwd_sweep",
    )(x, x, meta_tile, p["gmix"], p["w_fwd"], p["conv_w"], p["conv_b"], p["rgw_f"], p["rgb_f"], p["lam_f"],
      p["lbl_f"], p["mask_f"])


def _bwd_kernel(x_ref, xc_ref, q_ref, v_ref, hf_ref, of_ref, gmix_ref, w_ref, rgw_ref, rgb_ref, lam_ref,
                lbl_ref, mask_ref, hgg_ref, wa_ref, wb_ref, wo_ref,
                y_ref,
                st_scr, hc_scr, z_scr, a_scr, u_scr, ob_scr, ma_scr, ba_scr, bb_scr,
                k_scr, g_scr, b_scr, p_scr, x_scr, qe_scr, ke_scr, *, mxu_cumsum, two_pass):
    direct = True
    scan = _scan_linear_2pass if two_pass else _scan_linear
    tri = mask_ref[1 + len(FINE_LEVELS)].astype(BF16) if mxu_cumsum else None
    j = pl.program_id(1)

    @pl.when(j == 0)
    def _():
        st_scr[...] = jnp.zeros_like(st_scr)
        hc_scr[...] = jnp.zeros_like(hc_scr)

    hx = _rmsnorm(x_ref[...], gmix_ref[...]).astype(BF16)

    coef = _rglru_coef(lam_ref)
    for c in range(N_HEADS):
        a, u = _rglru_block(xc_ref[:, _lanes(c)], c, rgw_ref, rgb_ref, coef)
        a_scr[c] = a
        u_scr[c] = u
        if c % 2 == 1:
            i = c // 2
            for part in range(5):
                lo = part * D_MODEL + i * Z_CHUNK
                z_scr[:, lo:lo + Z_CHUNK] = _dot(hx, w_ref[:, lo:lo + Z_CHUNK])
    hc_scr[0:1, :] = scan(a_scr, u_scr, hc_scr[0:1, :], rev=True)

    lb = _lower_bound(lbl_ref[...])
    for c in range(N_HEADS):
        ls = _lanes(c)
        _forget_gate_block(z_scr[:, D_MODEL + c * D_HEAD:D_MODEL + (c + 1) * D_HEAD], lb, c, k_scr, g_scr, b_scr, tri)
        ba_scr[:, ls] = ((hf_ref[:, ls] + u_scr[c]) * _gelu_tanh(z_scr[:, ls])).astype(BF16)

    def branch_a_slice(i):
        def run():
            cols = slice(i * Z_CHUNK, (i + 1) * Z_CHUNK)
            gate = _sigmoid(z_scr[:, 3 * D_MODEL + i * Z_CHUNK:3 * D_MODEL + (i + 1) * Z_CHUNK])
            ma_scr[:, cols] = gate * _dot(ba_scr[...], wa_ref[:, cols])
        return run

    side_work = [branch_a_slice(i) for i in range(D_MODEL // Z_CHUNK)]
    if direct:
        _hgrn2_tile_direct(q_ref, k_scr, g_scr, b_scr, v_ref, p_scr, st_scr, mask_ref, ob_scr, rev=True,
                           side_work=side_work, prescanned=mxu_cumsum)
    else:
        _hgrn2_tile(q_ref, k_scr, g_scr, b_scr, v_ref, p_scr, x_scr, qe_scr, ke_scr, st_scr, mask_ref, ob_scr,
                    rev=True, side_work=side_work)

    for c in range(N_HEADS):
        ls = _lanes(c)
        o = of_ref[:, ls] + ob_scr[:, ls]
        o = o * lax.rsqrt(jnp.mean(o * o, axis=-1, keepdims=True) + EPS)
        og = z_scr[:, 2 * D_MODEL + c * D_HEAD:2 * D_MODEL + (c + 1) * D_HEAD]
        bb_scr[:, ls] = (o * hgg_ref[:, ls] * _silu(og)).astype(BF16)

    merged = ma_scr[...] + _sigmoid(z_scr[:, 4 * D_MODEL:5 * D_MODEL]) * _dot(bb_scr[...], wb_ref[...])
    y_ref[...] = x_ref[...] + _dot(merged.astype(BF16), wo_ref[...])


def _backward_sweep(x, xc, q, v, hf, of, p, mxu_cumsum, two_pass):
    bsz, t, _ = x.shape
    nt = t // TILE
    row_blk = lambda b, j: (b, nt - 1 - j, 0)
    const2 = lambda b, j: (0, 0)
    const3 = lambda b, j: (0, 0, 0)
    tok = pl.BlockSpec((None, TILE, D_MODEL), row_blk)
    sq = pl.BlockSpec((D_MODEL, D_MODEL), const2)
    vec = pl.BlockSpec((1, D_MODEL), const2)
    blk = (N_HEADS, TILE, D_HEAD)
    return pl.pallas_call(
        functools.partial(_bwd_kernel, mxu_cumsum=mxu_cumsum, two_pass=two_pass),
        grid=(bsz, nt),
        in_specs=[
            tok, tok, tok, tok, tok, tok,
            vec,
            pl.BlockSpec((D_MODEL, 5 * D_MODEL), const2),
            pl.BlockSpec((N_HEADS, D_HEAD, 2 * D_HEAD), const3),
            pl.BlockSpec((2, D_MODEL), const2),
            vec,
            pl.BlockSpec((2, D_MODEL), const2),
            pl.BlockSpec((2 + len(FINE_LEVELS), TILE, TILE), const3),
            vec, sq, sq, sq,
        ],
        out_specs=tok,
        out_shape=jax.ShapeDtypeStruct((bsz, t, D_MODEL), F32),
        scratch_shapes=[
            pltpu.VMEM((D_MODEL, D_HEAD), F32),
            pltpu.VMEM((SUBLANES, D_MODEL), F32),
            pltpu.VMEM((TILE, 5 * D_MODEL), F32),
            pltpu.VMEM(blk, F32),
            pltpu.VMEM(blk, F32),
            pltpu.VMEM((TILE, D_MODEL), F32),
            pltpu.VMEM((TILE, D_MODEL), F32),
            pltpu.VMEM((TILE, D_MODEL), BF16),
            pltpu.VMEM((TILE, D_MODEL), BF16),
        ] + _hgrn2_scratch(),
        compiler_params=pltpu.CompilerParams(
            dimension_semantics=("arbitrary", "arbitrary"), vmem_limit_bytes=VMEM_LIMIT),
        name="bwd_sweep",
    )(x, xc, q, v, hf, of, p["gmix"], p["w_bwd"], p["rgw_b"], p["rgb_b"], p["lam_b"], p["lbl_b"],
      p["mask_b"], p["hgg"], p["wa"], p["wb"], p["wo"])


def _mlp_kernel(x_ref, gmlp_ref, w1_ref, w2_ref, gfin_ref, y_ref):
    x = x_ref[...]
    hx = _rmsnorm(x, gmlp_ref[...]).astype(BF16)
    acc = x
    for c in range(D_FF // FF_CHUNK):
        cols = slice(c * FF_CHUNK, (c + 1) * FF_CHUNK)
        hm = jnp.maximum(_dot(hx, w1_ref[:, cols]), 0.0)
        acc = acc + _dot((hm * hm).astype(BF16), w2_ref[cols, :])
    y_ref[...] = _rmsnorm(acc, gfin_ref[...])


def _channel_mixer(x, p):
    bsz, t, _ = x.shape
    rows = bsz * t
    x2 = x.reshape(rows, D_MODEL)
    tok = pl.BlockSpec((MLP_TILE, D_MODEL), lambda i: (i, 0))
    const2 = lambda i: (0, 0)
    vec = pl.BlockSpec((1, D_MODEL), const2)
    y = pl.pallas_call(
        _mlp_kernel,
        grid=(rows // MLP_TILE,),
        in_specs=[tok, vec, pl.BlockSpec((D_MODEL, D_FF), const2), pl.BlockSpec((D_FF, D_MODEL), const2), vec],
        out_specs=tok,
        out_shape=jax.ShapeDtypeStruct((rows, D_MODEL), F32),
        compiler_params=pltpu.CompilerParams(
            dimension_semantics=("arbitrary",), vmem_limit_bytes=VMEM_LIMIT),
        name="channel_mixer",
    )(x2, p["gmlp"], p["w_mlp1"], p["w_mlp2"], p["gfin"])
    return y.reshape(bsz, t, D_MODEL)


def _prepare_params(meta_tokens, hg_lb_logits, norm_mix_g, w_in, conv_w, conv_b, rg_wa, rg_ba, rg_wx, rg_bx,
                    rg_lambda, hg_norm_g, w_branch_a, w_branch_b, w_out, norm_mlp_g, w_mlp1, w_mlp2,
                    final_norm_g):
    d = D_MODEL
    w = w_in[0].astype(BF16)
    col = lambda i: w[:, i * d:(i + 1) * d]
    row = lambda a: a.reshape(1, d).astype(F32)
    rgw = lambda k: jnp.concatenate([rg_wa[0, k], rg_wx[0, k]], axis=-1).astype(BF16)
    rgb = lambda k: jnp.stack([rg_ba[0, k], rg_bx[0, k]]).astype(F32)
    meta_tile = jnp.concatenate(
        [jnp.zeros((TILE - N_META, d), F32), meta_tokens.astype(F32)], axis=0)
    p = dict(
        gmix=row(norm_mix_g[0]),
        w_fwd=jnp.concatenate([col(0), col(2), col(3), col(5)], axis=1),
        w_bwd=jnp.concatenate([col(1), col(4), col(6), col(7), col(8)], axis=1),
        conv_w=conv_w[0].astype(F32), conv_b=row(conv_b[0]),
        rgw_f=rgw(0), rgw_b=rgw(1), rgb_f=rgb(0), rgb_b=rgb(1),
        lam_f=row(rg_lambda[0, 0]), lam_b=row(rg_lambda[0, 1]),
        lbl_f=hg_lb_logits[:, 0, :].astype(F32), lbl_b=hg_lb_logits[:, 1, :].astype(F32),
        mask_f=jnp.asarray(_fine_masks(TILE, False)), mask_b=jnp.asarray(_fine_masks(TILE, True)),
        hgg=row(hg_norm_g[0]),
        wa=w_branch_a[0].astype(BF16), wb=w_branch_b[0].astype(BF16), wo=w_out[0].astype(BF16),
        gmlp=row(norm_mlp_g[0]), w_mlp1=w_mlp1[0].astype(BF16), w_mlp2=w_mlp2[0].astype(BF16),
        gfin=row(final_norm_g),
    )
    return meta_tile, p


def _encode(x, meta_tile, p, fwd_cumsum, fwd_two_pass, bwd_cumsum, bwd_two_pass):
    assert x.shape[1] % TILE == 0 and (x.shape[0] * x.shape[1]) % MLP_TILE == 0
    xc, q, v, hf, of = _forward_sweep(x, meta_tile, p, fwd_cumsum, fwd_two_pass)
    x_mid = _backward_sweep(x, xc, q, v, hf, of, p, bwd_cumsum, bwd_two_pass)
    return _channel_mixer(x_mid, p)


def kernel(x_prompt, x_sample, meta_tokens, hg_lb_logits, norm_mix_g, w_in, conv_w, conv_b, rg_wa, rg_ba, rg_wx,
           rg_bx, rg_lambda, hg_norm_g, w_branch_a, w_branch_b, w_out, norm_mlp_g, w_mlp1, w_mlp2, final_norm_g):
    meta_tile, p = _prepare_params(meta_tokens, hg_lb_logits, norm_mix_g, w_in, conv_w, conv_b, rg_wa, rg_ba,
                                   rg_wx, rg_bx, rg_lambda, hg_norm_g, w_branch_a, w_branch_b, w_out,
                                   norm_mlp_g, w_mlp1, w_mlp2, final_norm_g)
    return (_encode(x_prompt, meta_tile, p, fwd_cumsum=True, fwd_two_pass=False, bwd_cumsum=False,
                    bwd_two_pass=True),
            _encode(x_sample, meta_tile, p, fwd_cumsum=False, fwd_two_pass=False, bwd_cumsum=False,
                    bwd_two_pass=False))
```

```python
import functools
from typing import NamedTuple

import numpy as np
import jax
import jax.numpy as jnp
from jax import lax
from jax.experimental import pallas as pl
from jax.experimental.pallas import tpu as pltpu

D_MODEL = 1024
N_META = 16
N_HEADS = 8
D_HEAD = D_MODEL // N_HEADS
D_FF = 4 * D_MODEL
RG_C = 8.0
EPS = 1e-6
LOG2_E = float(np.log2(np.e))

SUBLANES = 8
BF16_ROWS = 16
TILE = 128
SEG = TILE // SUBLANES
HALO = SUBLANES
MLP_TILE = 512
FF_CHUNK = 1024
Z_CHUNK = 256
VMEM_LIMIT = 56 * 1024 * 1024

F32 = jnp.float32
BF16 = jnp.bfloat16


def _level_half_sizes(n):
    out, m = [], 1
    while m < n:
        out.append(m)
        m *= 2
    return tuple(out)


LEVELS = _level_half_sizes(TILE)
FINE_LEVELS = tuple(m for m in LEVELS if m < SUBLANES)
COARSE_LEVELS = tuple(m for m in LEVELS if m >= SUBLANES)


def _fine_masks(n, rev):
    t = np.arange(n)[:, None]
    s = np.arange(n)[None, :]
    masks = [t == s]
    for m in FINE_LEVELS:
        same = (t // (2 * m)) == (s // (2 * m))
        t_up = (t % (2 * m)) >= m
        s_up = (s % (2 * m)) >= m
        masks.append(same & (~t_up) & s_up if rev else same & t_up & (~s_up))
    return np.stack(masks).astype(np.float32)


def _rmsnorm(x, g):
    ms = jnp.mean(x * x, axis=-1, keepdims=True)
    return x * lax.rsqrt(ms + EPS) * g


def _dot(a, b):
    return jnp.dot(a, b, preferred_element_type=F32)


def _dot_nt(a, b):
    return lax.dot_general(a, b, (((1,), (1,)), ((), ())), preferred_element_type=F32)


def _dot_tn(a, b):
    return lax.dot_general(a, b, (((0,), (0,)), ((), ())), preferred_element_type=F32)


def _gelu_tanh(x):
    c = np.float32(np.sqrt(2.0 / np.pi))
    return 0.5 * x * (1.0 + jnp.tanh(c * (x + 0.044715 * (x * x * x))))


def _sigmoid(x):
    return 0.5 * jnp.tanh(0.5 * x) + 0.5


def _silu(x):
    hx = 0.5 * x
    return hx * jnp.tanh(hx) + hx


def _lower_bound(lbl):
    mx = jnp.maximum(lbl[0:1, :], lbl[1:2, :])
    e0 = jnp.exp(lbl[0:1, :] - mx)
    e1 = jnp.exp(lbl[1:2, :] - mx)
    return e0 / (e0 + e1)


def _lanes(c):
    return slice(c * D_HEAD, (c + 1) * D_HEAD)


def _seg_order(n, rev):
    return range(n - 1, -1, -1) if rev else range(n)


def _scan_linear(a_ref, u_ref, carry, rev):
    hs = [jnp.zeros((SUBLANES, D_HEAD), F32)] * N_HEADS
    ps = [jnp.ones((SUBLANES, D_HEAD), F32)] * N_HEADS
    for gi in _seg_order(SEG, rev):
        idx = pl.ds(gi, SUBLANES, stride=SEG)
        for c in range(N_HEADS):
            a = a_ref[c, idx, :]
            hs[c] = a * hs[c] + u_ref[c, idx, :]
            ps[c] = a * ps[c]
            u_ref[c, idx, :] = hs[c]
            a_ref[c, idx, :] = ps[c]
    cs = [carry[:, _lanes(c)] for c in range(N_HEADS)]
    for r in _seg_order(SUBLANES, rev):
        rows = pl.ds(r * SEG, SEG)
        for c in range(N_HEADS):
            u_ref[c, rows, :] = u_ref[c, rows, :] + a_ref[c, rows, :] * cs[c]
            cs[c] = ps[c][r:r + 1, :] * cs[c] + hs[c][r:r + 1, :]
    return jnp.concatenate(cs, axis=1)


def _scan_sum(g_ref, rev):
    hs = [jnp.zeros((SUBLANES, D_HEAD), F32)] * N_HEADS
    for gi in _seg_order(SEG, rev):
        idx = pl.ds(gi, SUBLANES, stride=SEG)
        for c in range(N_HEADS):
            hs[c] = hs[c] + g_ref[c, idx, :]
            g_ref[c, idx, :] = hs[c]
    cs = [jnp.zeros((1, D_HEAD), F32)] * N_HEADS
    for r in _seg_order(SUBLANES, rev):
        rows = pl.ds(r * SEG, SEG)
        for c in range(N_HEADS):
            g_ref[c, rows, :] = g_ref[c, rows, :] + cs[c]
            cs[c] = cs[c] + hs[c][r:r + 1, :]


def _rglru_coef(lam_ref):
    nl = -lam_ref[...]
    return -RG_C * (jnp.maximum(nl, 0.0) + jnp.log1p(jnp.exp(-jnp.abs(nl))))


def _rglru_block(xc, c, rgw_ref, rgb_ref, coef):
    pre = _dot(xc.astype(BF16), rgw_ref[c])
    r = _sigmoid(pre[:, :D_HEAD] + rgb_ref[0:1, _lanes(c)])
    i = _sigmoid(pre[:, D_HEAD:] + rgb_ref[1:2, _lanes(c)])
    log_a = coef[:, _lanes(c)] * r
    a = jnp.exp(log_a)
    th = jnp.tanh(log_a)
    sq = -2.0 * th / (1.0 - th)
    mult = jnp.where(sq > 0.0, sq * lax.rsqrt(sq), 0.0)
    return a, mult * (i * xc)


def _forget_gate_block(ff, lb, c, k_ref, g_ref, b_ref):
    f = lb[:, _lanes(c)] + (1.0 - lb[:, _lanes(c)]) * _sigmoid(ff)
    g = jnp.log(f) * LOG2_E
    k_ref[c] = 1.0 - f
    g_ref[c] = g
    b_ref[c] = g


class _SeqRefs(NamedTuple):
    q: object
    k: object
    g: object
    b: object
    v: object
    p: object
    st: object
    o: object


def _hgrn2_tile(seq_refs, mask_ref, rev, side_work=()):
    n = TILE
    heads = range(N_HEADS)
    edge = 0 if rev else n - 1
    sub = lax.broadcasted_iota(jnp.int32, (n, D_HEAD), 0) % SUBLANES

    def boundary(b_ref, hd, m):
        parts = []
        for c in range(n // (2 * m)):
            r0 = c * 2 * m + (m if rev else m - 1)
            parts.append(jnp.broadcast_to(b_ref[hd, pl.ds(r0, 1), :], (2 * m, D_HEAD)))
        return jnp.concatenate(parts, axis=0) if len(parts) > 1 else parts[0]

    for r in seq_refs:
        _scan_sum(r.b, rev)

    for hd in heads:
        for r in seq_refs:
            qh = r.q[:, _lanes(hd)]
            kh = r.k[hd]
            gh = r.g[hd]
            acc = mask_ref[0] * _dot_nt(qh.astype(BF16), kh.astype(BF16))
            for li, m in enumerate(FINE_LEVELS):
                pos = sub % (2 * m)
                is_q = (pos < m) if rev else (pos >= m)
                if m == 1:
                    e = jnp.where(is_q, gh, 0.0)
                elif m == 2:
                    g_nxt = pltpu.roll(gh, n - 1, axis=0)
                    g_prv = pltpu.roll(gh, 1, axis=0)
                    if rev:
                        e = jnp.where(pos == 0, gh + g_nxt,
                                      jnp.where(pos == 1, gh, jnp.where(pos == 2, 0.0, g_prv)))
                    else:
                        e = jnp.where(pos == 0, g_nxt,
                                      jnp.where(pos == 1, 0.0, jnp.where(pos == 2, gh, gh + g_prv)))
                else:
                    e = -jnp.abs(r.b[hd] - boundary(r.b, hd, m))
                xl = (jnp.where(is_q, qh, kh) * jnp.exp2(e)).astype(BF16)
                acc = acc + mask_ref[1 + li] * _dot_nt(xl, xl)
            r.p[hd] = acc

    for ci, m in enumerate(COARSE_LEVELS):
        if ci < len(side_work):
            side_work[ci]()
        is_q_blk = lambda jb: (jb % 2 == 1) != rev
        groups = range(n // (2 * m))
        q_rows = [slice(c * 2 * m + (0 if rev else m), c * 2 * m + (m if rev else 2 * m)) for c in groups]
        k_cols = [slice(c * 2 * m + (m if rev else 0), c * 2 * m + (2 * m if rev else m)) for c in groups]
        for hd in heads:
            for r in seq_refs:
                qh = r.q[:, _lanes(hd)]
                kh = r.k[hd]
                sel = jnp.concatenate(
                    [(qh if is_q_blk(jb) else kh)[jb * m:(jb + 1) * m] for jb in range(n // m)], axis=0)
                xl = (sel * jnp.exp2(-jnp.abs(r.b[hd] - boundary(r.b, hd, m)))).astype(BF16)
                if m >= BF16_ROWS:
                    lhs = jnp.concatenate([xl[rs] for rs in q_rows], axis=0) if len(q_rows) > 1 \
                        else xl[q_rows[0]]
                    s = _dot_nt(lhs, xl)
                    s_rows = [slice(c * m, (c + 1) * m) for c in groups]
                else:
                    s = _dot_nt(xl, xl)
                    s_rows = q_rows
                for rs, sr, kc in zip(q_rows, s_rows, k_cols):
                    r.p[hd, rs, kc] = s[sr, kc]

    for hd in heads:
        for r in seq_refs:
            qh = r.q[:, _lanes(hd)]
            b = r.b[hd]
            b_edge = r.b[hd, pl.ds(edge, 1), :]
            qe = (qh * jnp.exp2(b)).astype(BF16)
            ke = (r.k[hd] * jnp.exp2(b_edge - b)).astype(BF16)
            st = r.st[_lanes(hd), :]
            vh = r.v[:, _lanes(hd)]
            r.o[:, _lanes(hd)] = _dot(r.p[hd].astype(BF16), vh) + _dot_nt(qe, st.astype(BF16))
            r.st[_lanes(hd), :] = st * jnp.exp2(b_edge) + _dot_tn(vh, ke)


def _hgrn2_scratch(seqs):
    blk = (seqs, N_HEADS, TILE, D_HEAD)
    return [
        pltpu.VMEM(blk, F32),
        pltpu.VMEM(blk, F32),
        pltpu.VMEM(blk, F32),
        pltpu.VMEM((seqs, N_HEADS, TILE, TILE), F32),
    ]


def _const_spec(shape):
    zeros = (0,) * len(shape)
    return pl.BlockSpec(shape, lambda b, j: zeros, pipeline_mode=pl.Buffered(1))


def _fwd_kernel(x_ref, xn_ref, meta_ref, gmix_ref, w_ref, convw_ref, convb_ref, rgw_ref, rgb_ref, lam_ref,
                lbl_ref, mask_ref,
                xc_ref, q_ref, v_ref, hf_ref, of_ref,
                st_scr, hc_scr, xa_scr, a_scr, u_scr, z_scr, k_scr, g_scr, b_scr, p_scr, *, seqs):
    j = pl.program_id(1)
    n_real = pl.num_programs(1) - 1
    is_meta = j == 0
    seq_ids = range(seqs)

    @pl.when(is_meta)
    def _():
        st_scr[...] = jnp.zeros_like(st_scr)
        hc_scr[...] = jnp.zeros_like(hc_scr)
        xa_scr[:, :, 0:HALO, :] = jnp.zeros((seqs, N_HEADS, HALO, D_HEAD), F32)

    mains = [jnp.where(is_meta, meta_ref[...], x_ref[s]) for s in seq_ids]
    halos = [jnp.where(j == n_real, 0.0, xn_ref[s]) for s in seq_ids]
    hx = _rmsnorm(jnp.concatenate(mains + halos, axis=0), gmix_ref[...]).astype(BF16)
    n_main = seqs * TILE
    hx_main = hx[0:n_main, :]

    for c in range(N_HEADS):
        ls = _lanes(c)
        if c % 2 == 0:
            xa = _dot(hx, w_ref[:, c * D_HEAD:(c + 2) * D_HEAD])
        cols = slice((c % 2) * D_HEAD, (c % 2 + 1) * D_HEAD)
        for s in seq_ids:
            xa_scr[s, c, HALO:HALO + TILE, :] = xa[s * TILE:(s + 1) * TILE, cols]
            xa_scr[s, c, HALO + TILE:, :] = xa[n_main + s * HALO:n_main + (s + 1) * HALO, cols]
            acc = convb_ref[:, ls] + convw_ref[0:1, ls] * xa_scr[s, c, pl.ds(HALO - 2, TILE), :]
            for tap in range(1, 4):
                acc = acc + convw_ref[tap:tap + 1, ls] * xa_scr[s, c, pl.ds(HALO - 2 + tap, TILE), :]
            a_scr[s, c] = acc
            xa_scr[s, c, 0:HALO, :] = xa_scr[s, c, TILE:TILE + HALO, :]

    coef = _rglru_coef(lam_ref)
    row = lax.broadcasted_iota(jnp.int32, (TILE, D_HEAD), 0)
    pad = jnp.logical_and(is_meta, row < TILE - N_META)
    for c in range(N_HEADS):
        for s in seq_ids:
            xc = a_scr[s, c]
            xc_ref[s, :, _lanes(c)] = xc
            a, u = _rglru_block(xc, c, rgw_ref, rgb_ref, coef)
            a_scr[s, c] = a
            u_scr[s, c] = jnp.where(pad, 0.0, u)
        if c % 2 == 1:
            i = c // 2
            for part in range(3):
                lo = part * D_MODEL + i * Z_CHUNK
                z_scr[:, lo:lo + Z_CHUNK] = _dot(hx_main, w_ref[:, D_MODEL + lo:D_MODEL + lo + Z_CHUNK])
    for s in seq_ids:
        hc_scr[s, 0:1, :] = _scan_linear(a_scr.at[s], u_scr.at[s], hc_scr[s, 0:1, :], rev=False)
    for c in range(N_HEADS):
        for s in seq_ids:
            hf_ref[s, :, _lanes(c)] = u_scr[s, c]

    lb = _lower_bound(lbl_ref[...])
    for c in range(N_HEADS):
        for s in seq_ids:
            rows = slice(s * TILE, (s + 1) * TILE)
            q_ref[s, :, _lanes(c)] = _silu(z_scr[rows, _lanes(c)])
            _forget_gate_block(z_scr[rows, D_MODEL + c * D_HEAD:D_MODEL + (c + 1) * D_HEAD], lb, c,
                               k_scr.at[s], g_scr.at[s], b_scr.at[s])
            v_ref[s, :, _lanes(c)] = z_scr[rows, 2 * D_MODEL + c * D_HEAD:2 * D_MODEL + (c + 1) * D_HEAD].astype(BF16)

    _hgrn2_tile([_SeqRefs(q_ref.at[s], k_scr.at[s], g_scr.at[s], b_scr.at[s], v_ref.at[s], p_scr.at[s],
                          st_scr.at[s], of_ref.at[s]) for s in seq_ids], mask_ref, rev=False)


def _forward_sweep(x, meta_tile, p, seqs):
    bsz, t, _ = x.shape
    assert bsz % seqs == 0
    nt = t // TILE
    halo_blocks = t // HALO
    tok = pl.BlockSpec((seqs, TILE, D_MODEL), lambda b, j: (b, jnp.maximum(j - 1, 0), 0))
    blk = (seqs, N_HEADS, TILE, D_HEAD)
    out_f32 = jax.ShapeDtypeStruct((bsz, t, D_MODEL), F32)
    return pl.pallas_call(
        functools.partial(_fwd_kernel, seqs=seqs),
        grid=(bsz // seqs, nt + 1),
        in_specs=[
            tok,
            pl.BlockSpec((seqs, HALO, D_MODEL),
                         lambda b, j: (b, jnp.minimum(j * (TILE // HALO), halo_blocks - 1), 0)),
            _const_spec((TILE, D_MODEL)),
            _const_spec((1, D_MODEL)),
            _const_spec((D_MODEL, 4 * D_MODEL)),
            _const_spec((4, D_MODEL)),
            _const_spec((1, D_MODEL)),
            _const_spec((N_HEADS, D_HEAD, 2 * D_HEAD)),
            _const_spec((2, D_MODEL)),
            _const_spec((1, D_MODEL)),
            _const_spec((2, D_MODEL)),
            _const_spec((1 + len(FINE_LEVELS), TILE, TILE)),
        ],
        out_specs=[tok, tok, tok, tok, tok],
        out_shape=[out_f32, out_f32, jax.ShapeDtypeStruct((bsz, t, D_MODEL), BF16), out_f32, out_f32],
        scratch_shapes=[
            pltpu.VMEM((seqs, D_MODEL, D_HEAD), F32),
            pltpu.VMEM((seqs, SUBLANES, D_MODEL), F32),
            pltpu.VMEM((seqs, N_HEADS, TILE + 2 * HALO, D_HEAD), F32),
            pltpu.VMEM(blk, F32),
            pltpu.VMEM(blk, F32),
            pltpu.VMEM((seqs * TILE, 3 * D_MODEL), F32),
        ] + _hgrn2_scratch(seqs),
        compiler_params=pltpu.CompilerParams(
            dimension_semantics=("arbitrary", "arbitrary"), vmem_limit_bytes=VMEM_LIMIT),
        name="fwd_sweep",
    )(x, x, meta_tile, p["gmix"], p["w_fwd"], p["conv_w"], p["conv_b"], p["rgw_f"], p["rgb_f"], p["lam_f"],
      p["lbl_f"], p["mask_f"])


def _bwd_kernel(x_ref, xc_ref, q_ref, v_ref, hf_ref, of_ref, gmix_ref, w_ref, rgw_ref, rgb_ref, lam_ref,
                lbl_ref, mask_ref, hgg_ref, wa_ref, wb_ref, wo_ref,
                y_ref,
                st_scr, hc_scr, z_scr, a_scr, u_scr, ob_scr, ma_scr, ba_scr, bb_scr,
                k_scr, g_scr, b_scr, p_scr, *, seqs):
    j = pl.program_id(1)
    seq_ids = range(seqs)

    @pl.when(j == 0)
    def _():
        st_scr[...] = jnp.zeros_like(st_scr)
        hc_scr[...] = jnp.zeros_like(hc_scr)

    x = jnp.concatenate([x_ref[s] for s in seq_ids], axis=0) if seqs > 1 else x_ref[0]
    hx = _rmsnorm(x, gmix_ref[...]).astype(BF16)

    coef = _rglru_coef(lam_ref)
    for c in range(N_HEADS):
        for s in seq_ids:
            a, u = _rglru_block(xc_ref[s, :, _lanes(c)], c, rgw_ref, rgb_ref, coef)
            a_scr[s, c] = a
            u_scr[s, c] = u
        if c % 2 == 1:
            i = c // 2
            for part in range(5):
                lo = part * D_MODEL + i * Z_CHUNK
                z_scr[:, lo:lo + Z_CHUNK] = _dot(hx, w_ref[:, lo:lo + Z_CHUNK])
    for s in seq_ids:
        hc_scr[s, 0:1, :] = _scan_linear(a_scr.at[s], u_scr.at[s], hc_scr[s, 0:1, :], rev=True)

    lb = _lower_bound(lbl_ref[...])
    for c in range(N_HEADS):
        ls = _lanes(c)
        for s in seq_ids:
            rows = slice(s * TILE, (s + 1) * TILE)
            _forget_gate_block(z_scr[rows, D_MODEL + c * D_HEAD:D_MODEL + (c + 1) * D_HEAD], lb, c,
                               k_scr.at[s], g_scr.at[s], b_scr.at[s])
            ba_scr[rows, ls] = ((hf_ref[s, :, ls] + u_scr[s, c]) * _gelu_tanh(z_scr[rows, ls])).astype(BF16)

    def branch_a_slice(i):
        def run():
            cols = slice(i * Z_CHUNK, (i + 1) * Z_CHUNK)
            gate = _sigmoid(z_scr[:, 3 * D_MODEL + i * Z_CHUNK:3 * D_MODEL + (i + 1) * Z_CHUNK])
            ma_scr[:, cols] = gate * _dot(ba_scr[...], wa_ref[:, cols])
        return run

    side_work = [branch_a_slice(i) for i in range(D_MODEL // Z_CHUNK)]
    _hgrn2_tile([_SeqRefs(q_ref.at[s], k_scr.at[s], g_scr.at[s], b_scr.at[s], v_ref.at[s], p_scr.at[s],
                          st_scr.at[s], ob_scr.at[pl.ds(s * TILE, TILE), :]) for s in seq_ids],
                mask_ref, rev=True, side_work=side_work)

    for c in range(N_HEADS):
        ls = _lanes(c)
        for s in seq_ids:
            rows = slice(s * TILE, (s + 1) * TILE)
            o = of_ref[s, :, ls] + ob_scr[rows, ls]
            o = o * lax.rsqrt(jnp.mean(o * o, axis=-1, keepdims=True) + EPS)
            og = z_scr[rows, 2 * D_MODEL + c * D_HEAD:2 * D_MODEL + (c + 1) * D_HEAD]
            bb_scr[rows, ls] = (o * hgg_ref[:, ls] * _silu(og)).astype(BF16)

    merged = ma_scr[...] + _sigmoid(z_scr[:, 4 * D_MODEL:5 * D_MODEL]) * _dot(bb_scr[...], wb_ref[...])
    y = x + _dot(merged.astype(BF16), wo_ref[...])
    for s in seq_ids:
        y_ref[s] = y[s * TILE:(s + 1) * TILE, :]


def _backward_sweep(x, xc, q, v, hf, of, p, seqs):
    bsz, t, _ = x.shape
    assert bsz % seqs == 0
    nt = t // TILE
    tok = pl.BlockSpec((seqs, TILE, D_MODEL), lambda b, j: (b, nt - 1 - j, 0))
    blk = (seqs, N_HEADS, TILE, D_HEAD)
    rows = seqs * TILE
    return pl.pallas_call(
        functools.partial(_bwd_kernel, seqs=seqs),
        grid=(bsz // seqs, nt),
        in_specs=[
            tok, tok, tok, tok, tok, tok,
            _const_spec((1, D_MODEL)),
            _const_spec((D_MODEL, 5 * D_MODEL)),
            _const_spec((N_HEADS, D_HEAD, 2 * D_HEAD)),
            _const_spec((2, D_MODEL)),
            _const_spec((1, D_MODEL)),
            _const_spec((2, D_MODEL)),
            _const_spec((1 + len(FINE_LEVELS), TILE, TILE)),
            _const_spec((1, D_MODEL)),
            _const_spec((D_MODEL, D_MODEL)),
            _const_spec((D_MODEL, D_MODEL)),
            _const_spec((D_MODEL, D_MODEL)),
        ],
        out_specs=tok,
        out_shape=jax.ShapeDtypeStruct((bsz, t, D_MODEL), F32),
        scratch_shapes=[
            pltpu.VMEM((seqs, D_MODEL, D_HEAD), F32),
            pltpu.VMEM((seqs, SUBLANES, D_MODEL), F32),
            pltpu.VMEM((rows, 5 * D_MODEL), F32),
            pltpu.VMEM(blk, F32),
            pltpu.VMEM(blk, F32),
            pltpu.VMEM((rows, D_MODEL), F32),
            pltpu.VMEM((rows, D_MODEL), F32),
            pltpu.VMEM((rows, D_MODEL), BF16),
            pltpu.VMEM((rows, D_MODEL), BF16),
        ] + _hgrn2_scratch(seqs),
        compiler_params=pltpu.CompilerParams(
            dimension_semantics=("arbitrary", "arbitrary"), vmem_limit_bytes=VMEM_LIMIT),
        name="bwd_sweep",
    )(x, xc, q, v, hf, of, p["gmix"], p["w_bwd"], p["rgw_b"], p["rgb_b"], p["lam_b"], p["lbl_b"],
      p["mask_b"], p["hgg"], p["wa"], p["wb"], p["wo"])


def _mlp_kernel(x_ref, gmlp_ref, w1_ref, w2_ref, gfin_ref, y_ref):
    x = x_ref[...]
    hx = _rmsnorm(x, gmlp_ref[...]).astype(BF16)
    acc = x
    for c in range(D_FF // FF_CHUNK):
        cols = slice(c * FF_CHUNK, (c + 1) * FF_CHUNK)
        hm = jnp.maximum(_dot(hx, w1_ref[:, cols]), 0.0)
        acc = acc + _dot((hm * hm).astype(BF16), w2_ref[cols, :])
    y_ref[...] = _rmsnorm(acc, gfin_ref[...])


def _channel_mixer(x, p):
    bsz, t, _ = x.shape
    rows = bsz * t
    x2 = x.reshape(rows, D_MODEL)
    tok = pl.BlockSpec((MLP_TILE, D_MODEL), lambda i: (i, 0))
    const2 = lambda i: (0, 0)
    vec = pl.BlockSpec((1, D_MODEL), const2)
    y = pl.pallas_call(
        _mlp_kernel,
        grid=(rows // MLP_TILE,),
        in_specs=[tok, vec, pl.BlockSpec((D_MODEL, D_FF), const2), pl.BlockSpec((D_FF, D_MODEL), const2), vec],
        out_specs=tok,
        out_shape=jax.ShapeDtypeStruct((rows, D_MODEL), F32),
        compiler_params=pltpu.CompilerParams(
            dimension_semantics=("arbitrary",), vmem_limit_bytes=VMEM_LIMIT),
        name="channel_mixer",
    )(x2, p["gmlp"], p["w_mlp1"], p["w_mlp2"], p["gfin"])
    return y.reshape(bsz, t, D_MODEL)


def _prepare_params(meta_tokens, hg_lb_logits, norm_mix_g, w_in, conv_w, conv_b, rg_wa, rg_ba, rg_wx, rg_bx,
                    rg_lambda, hg_norm_g, w_branch_a, w_branch_b, w_out, norm_mlp_g, w_mlp1, w_mlp2,
                    final_norm_g):
    d = D_MODEL
    w = w_in[0].astype(BF16)
    col = lambda i: w[:, i * d:(i + 1) * d]
    row = lambda a: a.reshape(1, d).astype(F32)
    rgw = lambda k: jnp.concatenate([rg_wa[0, k], rg_wx[0, k]], axis=-1).astype(BF16)
    rgb = lambda k: jnp.stack([rg_ba[0, k], rg_bx[0, k]]).astype(F32)
    meta_tile = jnp.concatenate(
        [jnp.zeros((TILE - N_META, d), F32), meta_tokens.astype(F32)], axis=0)
    p = dict(
        gmix=row(norm_mix_g[0]),
        w_fwd=jnp.concatenate([col(0), col(2), col(3), col(5)], axis=1),
        w_bwd=jnp.concatenate([col(1), col(4), col(6), col(7), col(8)], axis=1),
        conv_w=conv_w[0].astype(F32), conv_b=row(conv_b[0]),
        rgw_f=rgw(0), rgw_b=rgw(1), rgb_f=rgb(0), rgb_b=rgb(1),
        lam_f=row(rg_lambda[0, 0]), lam_b=row(rg_lambda[0, 1]),
        lbl_f=hg_lb_logits[:, 0, :].astype(F32), lbl_b=hg_lb_logits[:, 1, :].astype(F32),
        mask_f=jnp.asarray(_fine_masks(TILE, False)), mask_b=jnp.asarray(_fine_masks(TILE, True)),
        hgg=row(hg_norm_g[0]),
        wa=w_branch_a[0].astype(BF16), wb=w_branch_b[0].astype(BF16), wo=w_out[0].astype(BF16),
        gmlp=row(norm_mlp_g[0]), w_mlp1=w_mlp1[0].astype(BF16), w_mlp2=w_mlp2[0].astype(BF16),
        gfin=row(final_norm_g),
    )
    return meta_tile, p


def _encode(x, meta_tile, p, seqs):
    assert x.shape[1] % TILE == 0 and (x.shape[0] * x.shape[1]) % MLP_TILE == 0
    xc, q, v, hf, of = _forward_sweep(x, meta_tile, p, seqs)
    x_mid = _backward_sweep(x, xc, q, v, hf, of, p, seqs)
    return _channel_mixer(x_mid, p)


def kernel(x_prompt, x_sample, meta_tokens, hg_lb_logits, norm_mix_g, w_in, conv_w, conv_b, rg_wa, rg_ba, rg_wx,
           rg_bx, rg_lambda, hg_norm_g, w_branch_a, w_branch_b, w_out, norm_mlp_g, w_mlp1, w_mlp2, final_norm_g):
    meta_tile, p = _prepare_params(meta_tokens, hg_lb_logits, norm_mix_g, w_in, conv_w, conv_b, rg_wa, rg_ba,
                                   rg_wx, rg_bx, rg_lambda, hg_norm_g, w_branch_a, w_branch_b, w_out,
                                   norm_mlp_g, w_mlp1, w_mlp2, final_norm_g)
    return (_encode(x_prompt, meta_tile, p, seqs=2), _encode(x_sample, meta_tile, p, seqs=1))
```

```python
import functools
from typing import NamedTuple

import numpy as np
import jax
import jax.numpy as jnp
from jax import lax
from jax.experimental import pallas as pl
from jax.experimental.pallas import tpu as pltpu

D_MODEL = 1024
N_META = 16
N_HEADS = 8
D_HEAD = D_MODEL // N_HEADS
D_FF = 4 * D_MODEL
RG_C = 8.0
EPS = 1e-6
LOG2_E = float(np.log2(np.e))

SUBLANES = 8
BF16_ROWS = 16
TILE = 128
SEQS_PER_STEP = 2
SEG = TILE // SUBLANES
HALO = SUBLANES
MLP_TILE = 512
FF_CHUNK = 1024
Z_CHUNK = 256
VMEM_LIMIT = 56 * 1024 * 1024

F32 = jnp.float32
BF16 = jnp.bfloat16


def _level_half_sizes(n):
    out, m = [], 1
    while m < n:
        out.append(m)
        m *= 2
    return tuple(out)


LEVELS = _level_half_sizes(TILE)
FINE_LEVELS = tuple(m for m in LEVELS if m < SUBLANES)
COARSE_LEVELS = tuple(m for m in LEVELS if m >= SUBLANES)


def _fine_masks(n, rev):
    t = np.arange(n)[:, None]
    s = np.arange(n)[None, :]
    masks = [t == s]
    for m in FINE_LEVELS:
        same = (t // (2 * m)) == (s // (2 * m))
        t_up = (t % (2 * m)) >= m
        s_up = (s % (2 * m)) >= m
        masks.append(same & (~t_up) & s_up if rev else same & t_up & (~s_up))
    return np.stack(masks).astype(np.float32)


def _rmsnorm(x, g):
    ms = jnp.mean(x * x, axis=-1, keepdims=True)
    return x * lax.rsqrt(ms + EPS) * g


def _dot(a, b):
    return jnp.dot(a, b, preferred_element_type=F32)


def _dot_nt(a, b):
    return lax.dot_general(a, b, (((1,), (1,)), ((), ())), preferred_element_type=F32)


def _dot_tn(a, b):
    return lax.dot_general(a, b, (((0,), (0,)), ((), ())), preferred_element_type=F32)


def _gelu_tanh(x):
    c = np.float32(np.sqrt(2.0 / np.pi))
    return 0.5 * x * (1.0 + jnp.tanh(c * (x + 0.044715 * (x * x * x))))


def _sigmoid(x):
    return 0.5 * jnp.tanh(0.5 * x) + 0.5


def _silu(x):
    hx = 0.5 * x
    return hx * jnp.tanh(hx) + hx


def _lower_bound(lbl):
    mx = jnp.maximum(lbl[0:1, :], lbl[1:2, :])
    e0 = jnp.exp(lbl[0:1, :] - mx)
    e1 = jnp.exp(lbl[1:2, :] - mx)
    return e0 / (e0 + e1)


def _lanes(c):
    return slice(c * D_HEAD, (c + 1) * D_HEAD)


def _seg_order(n, rev):
    return range(n - 1, -1, -1) if rev else range(n)


def _scan_linear(a_ref, u_ref, carry, rev):
    hs = [jnp.zeros((SUBLANES, D_HEAD), F32)] * N_HEADS
    ps = [jnp.ones((SUBLANES, D_HEAD), F32)] * N_HEADS
    for gi in _seg_order(SEG, rev):
        idx = pl.ds(gi, SUBLANES, stride=SEG)
        for c in range(N_HEADS):
            a = a_ref[c, idx, :]
            hs[c] = a * hs[c] + u_ref[c, idx, :]
            ps[c] = a * ps[c]
            u_ref[c, idx, :] = hs[c]
            a_ref[c, idx, :] = ps[c]
    cs = [carry[:, _lanes(c)] for c in range(N_HEADS)]
    for r in _seg_order(SUBLANES, rev):
        rows = pl.ds(r * SEG, SEG)
        for c in range(N_HEADS):
            u_ref[c, rows, :] = u_ref[c, rows, :] + a_ref[c, rows, :] * cs[c]
            cs[c] = ps[c][r:r + 1, :] * cs[c] + hs[c][r:r + 1, :]
    return jnp.concatenate(cs, axis=1)


def _scan_sum(g_ref, rev):
    hs = [jnp.zeros((SUBLANES, D_HEAD), F32)] * N_HEADS
    for gi in _seg_order(SEG, rev):
        idx = pl.ds(gi, SUBLANES, stride=SEG)
        for c in range(N_HEADS):
            hs[c] = hs[c] + g_ref[c, idx, :]
            g_ref[c, idx, :] = hs[c]
    cs = [jnp.zeros((1, D_HEAD), F32)] * N_HEADS
    for r in _seg_order(SUBLANES, rev):
        rows = pl.ds(r * SEG, SEG)
        for c in range(N_HEADS):
            g_ref[c, rows, :] = g_ref[c, rows, :] + cs[c]
            cs[c] = cs[c] + hs[c][r:r + 1, :]


def _rglru_coef(lam_ref):
    nl = -lam_ref[...]
    return -RG_C * (jnp.maximum(nl, 0.0) + jnp.log1p(jnp.exp(-jnp.abs(nl))))


def _rglru_block(xc, c, rgw_ref, rgb_ref, coef):
    pre = _dot(xc.astype(BF16), rgw_ref[c])
    r = _sigmoid(pre[:, :D_HEAD] + rgb_ref[0:1, _lanes(c)])
    i = _sigmoid(pre[:, D_HEAD:] + rgb_ref[1:2, _lanes(c)])
    log_a = coef[:, _lanes(c)] * r
    a = jnp.exp(log_a)
    th = jnp.tanh(log_a)
    sq = -2.0 * th / (1.0 - th)
    mult = jnp.where(sq > 0.0, sq * lax.rsqrt(sq), 0.0)
    return a, mult * (i * xc)


def _forget_gate_block(ff, lb, c, k_ref, g_ref, b_ref):
    f = lb[:, _lanes(c)] + (1.0 - lb[:, _lanes(c)]) * _sigmoid(ff)
    g = jnp.log(f) * LOG2_E
    k_ref[c] = 1.0 - f
    g_ref[c] = g
    b_ref[c] = g


class _SeqRefs(NamedTuple):
    q: object
    k: object
    g: object
    b: object
    v: object
    p: object
    st: object
    o: object


def _hgrn2_tile(seq_refs, mask_ref, rev, side_work=()):
    n = TILE
    heads = range(N_HEADS)
    edge = 0 if rev else n - 1
    sub = lax.broadcasted_iota(jnp.int32, (n, D_HEAD), 0) % SUBLANES

    def boundary(b_ref, hd, m):
        parts = []
        for c in range(n // (2 * m)):
            r0 = c * 2 * m + (m if rev else m - 1)
            parts.append(jnp.broadcast_to(b_ref[hd, pl.ds(r0, 1), :], (2 * m, D_HEAD)))
        return jnp.concatenate(parts, axis=0) if len(parts) > 1 else parts[0]

    for r in seq_refs:
        _scan_sum(r.b, rev)

    for hd in heads:
        for r in seq_refs:
            qh = r.q[:, _lanes(hd)]
            kh = r.k[hd]
            gh = r.g[hd]
            acc = mask_ref[0] * _dot_nt(qh.astype(BF16), kh.astype(BF16))
            for li, m in enumerate(FINE_LEVELS):
                pos = sub % (2 * m)
                is_q = (pos < m) if rev else (pos >= m)
                if m == 1:
                    e = jnp.where(is_q, gh, 0.0)
                elif m == 2:
                    g_nxt = pltpu.roll(gh, n - 1, axis=0)
                    g_prv = pltpu.roll(gh, 1, axis=0)
                    if rev:
                        e = jnp.where(pos == 0, gh + g_nxt,
                                      jnp.where(pos == 1, gh, jnp.where(pos == 2, 0.0, g_prv)))
                    else:
                        e = jnp.where(pos == 0, g_nxt,
                                      jnp.where(pos == 1, 0.0, jnp.where(pos == 2, gh, gh + g_prv)))
                else:
                    e = -jnp.abs(r.b[hd] - boundary(r.b, hd, m))
                xl = (jnp.where(is_q, qh, kh) * jnp.exp2(e)).astype(BF16)
                acc = acc + mask_ref[1 + li] * _dot_nt(xl, xl)
            r.p[hd] = acc

    for ci, m in enumerate(COARSE_LEVELS):
        if ci < len(side_work):
            side_work[ci]()
        is_q_blk = lambda jb: (jb % 2 == 1) != rev
        groups = range(n // (2 * m))
        q_rows = [slice(c * 2 * m + (0 if rev else m), c * 2 * m + (m if rev else 2 * m)) for c in groups]
        k_cols = [slice(c * 2 * m + (m if rev else 0), c * 2 * m + (2 * m if rev else m)) for c in groups]
        for hd in heads:
            for r in seq_refs:
                qh = r.q[:, _lanes(hd)]
                kh = r.k[hd]
                sel = jnp.concatenate(
                    [(qh if is_q_blk(jb) else kh)[jb * m:(jb + 1) * m] for jb in range(n // m)], axis=0)
                xl = (sel * jnp.exp2(-jnp.abs(r.b[hd] - boundary(r.b, hd, m)))).astype(BF16)
                if m >= BF16_ROWS:
                    lhs = jnp.concatenate([xl[rs] for rs in q_rows], axis=0) if len(q_rows) > 1 \
                        else xl[q_rows[0]]
                    s = _dot_nt(lhs, xl)
                    s_rows = [slice(c * m, (c + 1) * m) for c in groups]
                else:
                    s = _dot_nt(xl, xl)
                    s_rows = q_rows
                for rs, sr, kc in zip(q_rows, s_rows, k_cols):
                    r.p[hd, rs, kc] = s[sr, kc]

    for hd in heads:
        for r in seq_refs:
            qh = r.q[:, _lanes(hd)]
            b = r.b[hd]
            b_edge = r.b[hd, pl.ds(edge, 1), :]
            qe = (qh * jnp.exp2(b)).astype(BF16)
            ke = (r.k[hd] * jnp.exp2(b_edge - b)).astype(BF16)
            st = r.st[_lanes(hd), :]
            vh = r.v[:, _lanes(hd)]
            r.o[:, _lanes(hd)] = _dot(r.p[hd].astype(BF16), vh) + _dot_nt(qe, st.astype(BF16))
            r.st[_lanes(hd), :] = st * jnp.exp2(b_edge) + _dot_tn(vh, ke)


def _hgrn2_scratch(seqs):
    blk = (seqs, N_HEADS, TILE, D_HEAD)
    return [
        pltpu.VMEM(blk, F32),
        pltpu.VMEM(blk, F32),
        pltpu.VMEM(blk, F32),
        pltpu.VMEM((seqs, N_HEADS, TILE, TILE), F32),
    ]


def _const_spec(shape):
    zeros = (0,) * len(shape)
    return pl.BlockSpec(shape, lambda b, j: zeros, pipeline_mode=pl.Buffered(1))


def _fwd_kernel(x_ref, xn_ref, meta_ref, gmix_ref, w_ref, convw_ref, convb_ref, rgw_ref, rgb_ref, lam_ref,
                lbl_ref, mask_ref,
                xc_ref, q_ref, v_ref, hf_ref, of_ref,
                st_scr, hc_scr, xa_scr, a_scr, u_scr, z_scr, k_scr, g_scr, b_scr, p_scr, *, seqs):
    j = pl.program_id(1)
    n_real = pl.num_programs(1) - 1
    is_meta = j == 0
    seq_ids = range(seqs)

    @pl.when(is_meta)
    def _():
        st_scr[...] = jnp.zeros_like(st_scr)
        hc_scr[...] = jnp.zeros_like(hc_scr)
        xa_scr[:, :, 0:HALO, :] = jnp.zeros((seqs, N_HEADS, HALO, D_HEAD), F32)

    mains = [jnp.where(is_meta, meta_ref[...], x_ref[s]) for s in seq_ids]
    halos = [jnp.where(j == n_real, 0.0, xn_ref[s]) for s in seq_ids]
    hx = _rmsnorm(jnp.concatenate(mains + halos, axis=0), gmix_ref[...]).astype(BF16)
    n_main = seqs * TILE
    hx_main = hx[0:n_main, :]

    for c in range(N_HEADS):
        ls = _lanes(c)
        if c % 2 == 0:
            xa = _dot(hx, w_ref[:, c * D_HEAD:(c + 2) * D_HEAD])
        cols = slice((c % 2) * D_HEAD, (c % 2 + 1) * D_HEAD)
        for s in seq_ids:
            xa_scr[s, c, HALO:HALO + TILE, :] = xa[s * TILE:(s + 1) * TILE, cols]
            xa_scr[s, c, HALO + TILE:, :] = xa[n_main + s * HALO:n_main + (s + 1) * HALO, cols]
            acc = convb_ref[:, ls] + convw_ref[0:1, ls] * xa_scr[s, c, pl.ds(HALO - 2, TILE), :]
            for tap in range(1, 4):
                acc = acc + convw_ref[tap:tap + 1, ls] * xa_scr[s, c, pl.ds(HALO - 2 + tap, TILE), :]
            a_scr[s, c] = acc
            xa_scr[s, c, 0:HALO, :] = xa_scr[s, c, TILE:TILE + HALO, :]

    coef = _rglru_coef(lam_ref)
    row = lax.broadcasted_iota(jnp.int32, (TILE, D_HEAD), 0)
    pad = jnp.logical_and(is_meta, row < TILE - N_META)
    for c in range(N_HEADS):
        for s in seq_ids:
            xc = a_scr[s, c]
            xc_ref[s, :, _lanes(c)] = xc
            a, u = _rglru_block(xc, c, rgw_ref, rgb_ref, coef)
            a_scr[s, c] = a
            u_scr[s, c] = jnp.where(pad, 0.0, u)
        if c % 2 == 1:
            i = c // 2
            for part in range(3):
                lo = part * D_MODEL + i * Z_CHUNK
                z_scr[:, lo:lo + Z_CHUNK] = _dot(hx_main, w_ref[:, D_MODEL + lo:D_MODEL + lo + Z_CHUNK])
    for s in seq_ids:
        hc_scr[s, 0:1, :] = _scan_linear(a_scr.at[s], u_scr.at[s], hc_scr[s, 0:1, :], rev=False)
    for c in range(N_HEADS):
        for s in seq_ids:
            hf_ref[s, :, _lanes(c)] = u_scr[s, c]

    lb = _lower_bound(lbl_ref[...])
    for c in range(N_HEADS):
        for s in seq_ids:
            rows = slice(s * TILE, (s + 1) * TILE)
            q_ref[s, :, _lanes(c)] = _silu(z_scr[rows, _lanes(c)])
            _forget_gate_block(z_scr[rows, D_MODEL + c * D_HEAD:D_MODEL + (c + 1) * D_HEAD], lb, c,
                               k_scr.at[s], g_scr.at[s], b_scr.at[s])
            v_ref[s, :, _lanes(c)] = z_scr[rows, 2 * D_MODEL + c * D_HEAD:2 * D_MODEL + (c + 1) * D_HEAD].astype(BF16)

    _hgrn2_tile([_SeqRefs(q_ref.at[s], k_scr.at[s], g_scr.at[s], b_scr.at[s], v_ref.at[s], p_scr.at[s],
                          st_scr.at[s], of_ref.at[s]) for s in seq_ids], mask_ref, rev=False)


def _forward_sweep(x, meta_tile, p, seqs):
    bsz, t, _ = x.shape
    assert bsz % seqs == 0
    nt = t // TILE
    halo_blocks = t // HALO
    tok = pl.BlockSpec((seqs, TILE, D_MODEL), lambda b, j: (b, jnp.maximum(j - 1, 0), 0))
    blk = (seqs, N_HEADS, TILE, D_HEAD)
    out_f32 = jax.ShapeDtypeStruct((bsz, t, D_MODEL), F32)
    return pl.pallas_call(
        functools.partial(_fwd_kernel, seqs=seqs),
        grid=(bsz // seqs, nt + 1),
        in_specs=[
            tok,
            pl.BlockSpec((seqs, HALO, D_MODEL),
                         lambda b, j: (b, jnp.minimum(j * (TILE // HALO), halo_blocks - 1), 0)),
            _const_spec((TILE, D_MODEL)),
            _const_spec((1, D_MODEL)),
            _const_spec((D_MODEL, 4 * D_MODEL)),
            _const_spec((4, D_MODEL)),
            _const_spec((1, D_MODEL)),
            _const_spec((N_HEADS, D_HEAD, 2 * D_HEAD)),
            _const_spec((2, D_MODEL)),
            _const_spec((1, D_MODEL)),
            _const_spec((2, D_MODEL)),
            _const_spec((1 + len(FINE_LEVELS), TILE, TILE)),
        ],
        out_specs=[tok, tok, tok, tok, tok],
        out_shape=[out_f32, out_f32, jax.ShapeDtypeStruct((bsz, t, D_MODEL), BF16), out_f32, out_f32],
        scratch_shapes=[
            pltpu.VMEM((seqs, D_MODEL, D_HEAD), F32),
            pltpu.VMEM((seqs, SUBLANES, D_MODEL), F32),
            pltpu.VMEM((seqs, N_HEADS, TILE + 2 * HALO, D_HEAD), F32),
            pltpu.VMEM(blk, F32),
            pltpu.VMEM(blk, F32),
            pltpu.VMEM((seqs * TILE, 3 * D_MODEL), F32),
        ] + _hgrn2_scratch(seqs),
        compiler_params=pltpu.CompilerParams(
            dimension_semantics=("arbitrary", "arbitrary"), vmem_limit_bytes=VMEM_LIMIT),
        name="fwd_sweep",
    )(x, x, meta_tile, p["gmix"], p["w_fwd"], p["conv_w"], p["conv_b"], p["rgw_f"], p["rgb_f"], p["lam_f"],
      p["lbl_f"], p["mask_f"])


def _bwd_kernel(x_ref, xc_ref, q_ref, v_ref, hf_ref, of_ref, gmix_ref, w_ref, rgw_ref, rgb_ref, lam_ref,
                lbl_ref, mask_ref, hgg_ref, wa_ref, wb_ref, wo_ref,
                y_ref,
                st_scr, hc_scr, z_scr, a_scr, u_scr, ob_scr, ma_scr, ba_scr, bb_scr,
                k_scr, g_scr, b_scr, p_scr, *, seqs):
    j = pl.program_id(1)
    seq_ids = range(seqs)

    @pl.when(j == 0)
    def _():
        st_scr[...] = jnp.zeros_like(st_scr)
        hc_scr[...] = jnp.zeros_like(hc_scr)

    x = jnp.concatenate([x_ref[s] for s in seq_ids], axis=0) if seqs > 1 else x_ref[0]
    hx = _rmsnorm(x, gmix_ref[...]).astype(BF16)

    coef = _rglru_coef(lam_ref)
    for c in range(N_HEADS):
        for s in seq_ids:
            a, u = _rglru_block(xc_ref[s, :, _lanes(c)], c, rgw_ref, rgb_ref, coef)
            a_scr[s, c] = a
            u_scr[s, c] = u
        if c % 2 == 1:
            i = c // 2
            for part in range(5):
                lo = part * D_MODEL + i * Z_CHUNK
                z_scr[:, lo:lo + Z_CHUNK] = _dot(hx, w_ref[:, lo:lo + Z_CHUNK])
    for s in seq_ids:
        hc_scr[s, 0:1, :] = _scan_linear(a_scr.at[s], u_scr.at[s], hc_scr[s, 0:1, :], rev=True)

    lb = _lower_bound(lbl_ref[...])
    for c in range(N_HEADS):
        ls = _lanes(c)
        for s in seq_ids:
            rows = slice(s * TILE, (s + 1) * TILE)
            _forget_gate_block(z_scr[rows, D_MODEL + c * D_HEAD:D_MODEL + (c + 1) * D_HEAD], lb, c,
                               k_scr.at[s], g_scr.at[s], b_scr.at[s])
            ba_scr[rows, ls] = ((hf_ref[s, :, ls] + u_scr[s, c]) * _gelu_tanh(z_scr[rows, ls])).astype(BF16)

    def branch_a_slice(i):
        def run():
            cols = slice(i * Z_CHUNK, (i + 1) * Z_CHUNK)
            gate = _sigmoid(z_scr[:, 3 * D_MODEL + i * Z_CHUNK:3 * D_MODEL + (i + 1) * Z_CHUNK])
            ma_scr[:, cols] = gate * _dot(ba_scr[...], wa_ref[:, cols])
        return run

    side_work = [branch_a_slice(i) for i in range(D_MODEL // Z_CHUNK)]
    _hgrn2_tile([_SeqRefs(q_ref.at[s], k_scr.at[s], g_scr.at[s], b_scr.at[s], v_ref.at[s], p_scr.at[s],
                          st_scr.at[s], ob_scr.at[pl.ds(s * TILE, TILE), :]) for s in seq_ids],
                mask_ref, rev=True, side_work=side_work)

    for c in range(N_HEADS):
        ls = _lanes(c)
        for s in seq_ids:
            rows = slice(s * TILE, (s + 1) * TILE)
            o = of_ref[s, :, ls] + ob_scr[rows, ls]
            o = o * lax.rsqrt(jnp.mean(o * o, axis=-1, keepdims=True) + EPS)
            og = z_scr[rows, 2 * D_MODEL + c * D_HEAD:2 * D_MODEL + (c + 1) * D_HEAD]
            bb_scr[rows, ls] = (o * hgg_ref[:, ls] * _silu(og)).astype(BF16)

    merged = ma_scr[...] + _sigmoid(z_scr[:, 4 * D_MODEL:5 * D_MODEL]) * _dot(bb_scr[...], wb_ref[...])
    y = x + _dot(merged.astype(BF16), wo_ref[...])
    for s in seq_ids:
        y_ref[s] = y[s * TILE:(s + 1) * TILE, :]


def _backward_sweep(x, xc, q, v, hf, of, p, seqs):
    bsz, t, _ = x.shape
    assert bsz % seqs == 0
    nt = t // TILE
    tok = pl.BlockSpec((seqs, TILE, D_MODEL), lambda b, j: (b, nt - 1 - j, 0))
    blk = (seqs, N_HEADS, TILE, D_HEAD)
    rows = seqs * TILE
    return pl.pallas_call(
        functools.partial(_bwd_kernel, seqs=seqs),
        grid=(bsz // seqs, nt),
        in_specs=[
            tok, tok, tok, tok, tok, tok,
            _const_spec((1, D_MODEL)),
            _const_spec((D_MODEL, 5 * D_MODEL)),
            _const_spec((N_HEADS, D_HEAD, 2 * D_HEAD)),
            _const_spec((2, D_MODEL)),
            _const_spec((1, D_MODEL)),
            _const_spec((2, D_MODEL)),
            _const_spec((1 + len(FINE_LEVELS), TILE, TILE)),
            _const_spec((1, D_MODEL)),
            _const_spec((D_MODEL, D_MODEL)),
            _const_spec((D_MODEL, D_MODEL)),
            _const_spec((D_MODEL, D_MODEL)),
        ],
        out_specs=tok,
        out_shape=jax.ShapeDtypeStruct((bsz, t, D_MODEL), F32),
        scratch_shapes=[
            pltpu.VMEM((seqs, D_MODEL, D_HEAD), F32),
            pltpu.VMEM((seqs, SUBLANES, D_MODEL), F32),
            pltpu.VMEM((rows, 5 * D_MODEL), F32),
            pltpu.VMEM(blk, F32),
            pltpu.VMEM(blk, F32),
            pltpu.VMEM((rows, D_MODEL), F32),
            pltpu.VMEM((rows, D_MODEL), F32),
            pltpu.VMEM((rows, D_MODEL), BF16),
            pltpu.VMEM((rows, D_MODEL), BF16),
        ] + _hgrn2_scratch(seqs),
        compiler_params=pltpu.CompilerParams(
            dimension_semantics=("arbitrary", "arbitrary"), vmem_limit_bytes=VMEM_LIMIT),
        name="bwd_sweep",
    )(x, xc, q, v, hf, of, p["gmix"], p["w_bwd"], p["rgw_b"], p["rgb_b"], p["lam_b"], p["lbl_b"],
      p["mask_b"], p["hgg"], p["wa"], p["wb"], p["wo"])


def _mlp_kernel(x_ref, gmlp_ref, w1_ref, w2_ref, gfin_ref, y_ref):
    x = x_ref[...]
    hx = _rmsnorm(x, gmlp_ref[...]).astype(BF16)
    acc = x
    for c in range(D_FF // FF_CHUNK):
        cols = slice(c * FF_CHUNK, (c + 1) * FF_CHUNK)
        hm = jnp.maximum(_dot(hx, w1_ref[:, cols]), 0.0)
        acc = acc + _dot((hm * hm).astype(BF16), w2_ref[cols, :])
    y_ref[...] = _rmsnorm(acc, gfin_ref[...])


def _channel_mixer(x, p):
    bsz, t, _ = x.shape
    rows = bsz * t
    x2 = x.reshape(rows, D_MODEL)
    tok = pl.BlockSpec((MLP_TILE, D_MODEL), lambda i: (i, 0))
    const2 = lambda i: (0, 0)
    vec = pl.BlockSpec((1, D_MODEL), const2)
    y = pl.pallas_call(
        _mlp_kernel,
        grid=(rows // MLP_TILE,),
        in_specs=[tok, vec, pl.BlockSpec((D_MODEL, D_FF), const2), pl.BlockSpec((D_FF, D_MODEL), const2), vec],
        out_specs=tok,
        out_shape=jax.ShapeDtypeStruct((rows, D_MODEL), F32),
        compiler_params=pltpu.CompilerParams(
            dimension_semantics=("arbitrary",), vmem_limit_bytes=VMEM_LIMIT),
        name="channel_mixer",
    )(x2, p["gmlp"], p["w_mlp1"], p["w_mlp2"], p["gfin"])
    return y.reshape(bsz, t, D_MODEL)


def _prepare_params(meta_tokens, hg_lb_logits, norm_mix_g, w_in, conv_w, conv_b, rg_wa, rg_ba, rg_wx, rg_bx,
                    rg_lambda, hg_norm_g, w_branch_a, w_branch_b, w_out, norm_mlp_g, w_mlp1, w_mlp2,
                    final_norm_g):
    d = D_MODEL
    w = w_in[0].astype(BF16)
    col = lambda i: w[:, i * d:(i + 1) * d]
    row = lambda a: a.reshape(1, d).astype(F32)
    rgw = lambda k: jnp.concatenate([rg_wa[0, k], rg_wx[0, k]], axis=-1).astype(BF16)
    rgb = lambda k: jnp.stack([rg_ba[0, k], rg_bx[0, k]]).astype(F32)
    meta_tile = jnp.concatenate(
        [jnp.zeros((TILE - N_META, d), F32), meta_tokens.astype(F32)], axis=0)
    p = dict(
        gmix=row(norm_mix_g[0]),
        w_fwd=jnp.concatenate([col(0), col(2), col(3), col(5)], axis=1),
        w_bwd=jnp.concatenate([col(1), col(4), col(6), col(7), col(8)], axis=1),
        conv_w=conv_w[0].astype(F32), conv_b=row(conv_b[0]),
        rgw_f=rgw(0), rgw_b=rgw(1), rgb_f=rgb(0), rgb_b=rgb(1),
        lam_f=row(rg_lambda[0, 0]), lam_b=row(rg_lambda[0, 1]),
        lbl_f=hg_lb_logits[:, 0, :].astype(F32), lbl_b=hg_lb_logits[:, 1, :].astype(F32),
        mask_f=jnp.asarray(_fine_masks(TILE, False)), mask_b=jnp.asarray(_fine_masks(TILE, True)),
        hgg=row(hg_norm_g[0]),
        wa=w_branch_a[0].astype(BF16), wb=w_branch_b[0].astype(BF16), wo=w_out[0].astype(BF16),
        gmlp=row(norm_mlp_g[0]), w_mlp1=w_mlp1[0].astype(BF16), w_mlp2=w_mlp2[0].astype(BF16),
        gfin=row(final_norm_g),
    )
    return meta_tile, p


def _encode(x, meta_tile, p, seqs):
    assert x.shape[1] % TILE == 0 and (x.shape[0] * x.shape[1]) % MLP_TILE == 0
    xc, q, v, hf, of = _forward_sweep(x, meta_tile, p, seqs)
    x_mid = _backward_sweep(x, xc, q, v, hf, of, p, seqs)
    return _channel_mixer(x_mid, p)


def kernel(x_prompt, x_sample, meta_tokens, hg_lb_logits, norm_mix_g, w_in, conv_w, conv_b, rg_wa, rg_ba, rg_wx,
           rg_bx, rg_lambda, hg_norm_g, w_branch_a, w_branch_b, w_out, norm_mlp_g, w_mlp1, w_mlp2, final_norm_g):
    meta_tile, p = _prepare_params(meta_tokens, hg_lb_logits, norm_mix_g, w_in, conv_w, conv_b, rg_wa, rg_ba,
                                   rg_wx, rg_bx, rg_lambda, hg_norm_g, w_branch_a, w_branch_b, w_out,
                                   norm_mlp_g, w_mlp1, w_mlp2, final_norm_g)
    return (_encode(x_prompt, meta_tile, p, SEQS_PER_STEP), _encode(x_sample, meta_tile, p, SEQS_PER_STEP))
```

```python
import functools
from typing import NamedTuple

import numpy as np
import jax
import jax.numpy as jnp
from jax import lax
from jax.experimental import pallas as pl
from jax.experimental.pallas import tpu as pltpu

D_MODEL = 1024
N_META = 16
N_HEADS = 8
D_HEAD = D_MODEL // N_HEADS
D_FF = 4 * D_MODEL
RG_C = 8.0
EPS = 1e-6
LOG2_E = float(np.log2(np.e))

SUBLANES = 8
BF16_ROWS = 16
TILE = 128
SEQS_PER_STEP = 2
SEG = TILE // SUBLANES
HALO = SUBLANES
MLP_TILE = 512
FF_CHUNK = 1024
Z_CHUNK = 256
VMEM_LIMIT = 56 * 1024 * 1024

F32 = jnp.float32
BF16 = jnp.bfloat16


def _level_half_sizes(n):
    out, m = [], 1
    while m < n:
        out.append(m)
        m *= 2
    return tuple(out)


LEVELS = _level_half_sizes(TILE)
FINE_LEVELS = tuple(m for m in LEVELS if m < SUBLANES)
COARSE_LEVELS = tuple(m for m in LEVELS if m >= SUBLANES)


def _fine_masks(n, rev):
    t = np.arange(n)[:, None]
    s = np.arange(n)[None, :]
    masks = [t == s]
    for m in FINE_LEVELS:
        same = (t // (2 * m)) == (s // (2 * m))
        t_up = (t % (2 * m)) >= m
        s_up = (s % (2 * m)) >= m
        masks.append(same & (~t_up) & s_up if rev else same & t_up & (~s_up))
    return np.stack(masks).astype(np.float32)


def _rmsnorm(x, g):
    ms = jnp.mean(x * x, axis=-1, keepdims=True)
    return x * lax.rsqrt(ms + EPS) * g


def _dot(a, b):
    return jnp.dot(a, b, preferred_element_type=F32)


def _dot_nt(a, b):
    return lax.dot_general(a, b, (((1,), (1,)), ((), ())), preferred_element_type=F32)


def _dot_tn(a, b):
    return lax.dot_general(a, b, (((0,), (0,)), ((), ())), preferred_element_type=F32)


def _gelu_tanh(x):
    c = np.float32(np.sqrt(2.0 / np.pi))
    return 0.5 * x * (1.0 + jnp.tanh(c * (x + 0.044715 * (x * x * x))))


def _sigmoid(x):
    return 0.5 * jnp.tanh(0.5 * x) + 0.5


def _silu(x):
    hx = 0.5 * x
    return hx * jnp.tanh(hx) + hx


def _lower_bound(lbl):
    mx = jnp.maximum(lbl[0:1, :], lbl[1:2, :])
    e0 = jnp.exp(lbl[0:1, :] - mx)
    e1 = jnp.exp(lbl[1:2, :] - mx)
    return e0 / (e0 + e1)


def _lanes(c):
    return slice(c * D_HEAD, (c + 1) * D_HEAD)


def _seg_order(n, rev):
    return range(n - 1, -1, -1) if rev else range(n)


def _scan_linear(a_ref, u_ref, carry, rev):
    hs = [jnp.zeros((SUBLANES, D_HEAD), F32)] * N_HEADS
    ps = [jnp.ones((SUBLANES, D_HEAD), F32)] * N_HEADS
    for gi in _seg_order(SEG, rev):
        idx = pl.ds(gi, SUBLANES, stride=SEG)
        for c in range(N_HEADS):
            a = a_ref[c, idx, :]
            hs[c] = a * hs[c] + u_ref[c, idx, :]
            ps[c] = a * ps[c]
            u_ref[c, idx, :] = hs[c]
            a_ref[c, idx, :] = ps[c]
    cs = [carry[:, _lanes(c)] for c in range(N_HEADS)]
    for r in _seg_order(SUBLANES, rev):
        rows = pl.ds(r * SEG, SEG)
        for c in range(N_HEADS):
            u_ref[c, rows, :] = u_ref[c, rows, :] + a_ref[c, rows, :] * cs[c]
            cs[c] = ps[c][r:r + 1, :] * cs[c] + hs[c][r:r + 1, :]
    return jnp.concatenate(cs, axis=1)


def _scan_sum(g_ref, rev):
    hs = [jnp.zeros((SUBLANES, D_HEAD), F32)] * N_HEADS
    for gi in _seg_order(SEG, rev):
        idx = pl.ds(gi, SUBLANES, stride=SEG)
        for c in range(N_HEADS):
            hs[c] = hs[c] + g_ref[c, idx, :]
            g_ref[c, idx, :] = hs[c]
    cs = [jnp.zeros((1, D_HEAD), F32)] * N_HEADS
    for r in _seg_order(SUBLANES, rev):
        rows = pl.ds(r * SEG, SEG)
        for c in range(N_HEADS):
            g_ref[c, rows, :] = g_ref[c, rows, :] + cs[c]
            cs[c] = cs[c] + hs[c][r:r + 1, :]


def _rglru_coef(lam_ref):
    nl = -lam_ref[...]
    return -RG_C * (jnp.maximum(nl, 0.0) + jnp.log1p(jnp.exp(-jnp.abs(nl))))


def _rglru_block(xc, c, rgw_ref, rgb_ref, coef):
    pre = _dot(xc.astype(BF16), rgw_ref[c])
    r = _sigmoid(pre[:, :D_HEAD] + rgb_ref[0:1, _lanes(c)])
    i = _sigmoid(pre[:, D_HEAD:] + rgb_ref[1:2, _lanes(c)])
    log_a = coef[:, _lanes(c)] * r
    a = jnp.exp(log_a)
    th = jnp.tanh(log_a)
    sq = -2.0 * th / (1.0 - th)
    mult = jnp.where(sq > 0.0, sq * lax.rsqrt(sq), 0.0)
    return a, mult * (i * xc)


def _forget_gate_block(ff, lb, c, k_ref, g_ref, b_ref):
    f = lb[:, _lanes(c)] + (1.0 - lb[:, _lanes(c)]) * _sigmoid(ff)
    g = jnp.log(f) * LOG2_E
    k_ref[c] = 1.0 - f
    g_ref[c] = g
    b_ref[c] = g


class _SeqRefs(NamedTuple):
    q: object
    k: object
    g: object
    b: object
    v: object
    p: object
    st: object
    o: object


def _hgrn2_tile(seq_refs, mask_ref, rev, side_work=(), emit=None):
    n = TILE
    heads = range(N_HEADS)
    edge = 0 if rev else n - 1
    sub = lax.broadcasted_iota(jnp.int32, (n, D_HEAD), 0) % SUBLANES

    def boundary(b_ref, hd, m):
        parts = []
        for c in range(n // (2 * m)):
            r0 = c * 2 * m + (m if rev else m - 1)
            parts.append(jnp.broadcast_to(b_ref[hd, pl.ds(r0, 1), :], (2 * m, D_HEAD)))
        return jnp.concatenate(parts, axis=0) if len(parts) > 1 else parts[0]

    for r in seq_refs:
        _scan_sum(r.b, rev)

    for hd in heads:
        for r in seq_refs:
            qh = r.q[:, _lanes(hd)]
            kh = r.k[hd]
            gh = r.g[hd]
            acc = mask_ref[0] * _dot_nt(qh.astype(BF16), kh.astype(BF16))
            for li, m in enumerate(FINE_LEVELS):
                pos = sub % (2 * m)
                is_q = (pos < m) if rev else (pos >= m)
                if m == 1:
                    e = jnp.where(is_q, gh, 0.0)
                elif m == 2:
                    g_nxt = pltpu.roll(gh, n - 1, axis=0)
                    g_prv = pltpu.roll(gh, 1, axis=0)
                    if rev:
                        e = jnp.where(pos == 0, gh + g_nxt,
                                      jnp.where(pos == 1, gh, jnp.where(pos == 2, 0.0, g_prv)))
                    else:
                        e = jnp.where(pos == 0, g_nxt,
                                      jnp.where(pos == 1, 0.0, jnp.where(pos == 2, gh, gh + g_prv)))
                else:
                    e = -jnp.abs(r.b[hd] - boundary(r.b, hd, m))
                xl = (jnp.where(is_q, qh, kh) * jnp.exp2(e)).astype(BF16)
                acc = acc + mask_ref[1 + li] * _dot_nt(xl, xl)
            r.p[hd] = acc

    for ci, m in enumerate(COARSE_LEVELS):
        if ci < len(side_work):
            side_work[ci]()
        is_q_blk = lambda jb: (jb % 2 == 1) != rev
        groups = range(n // (2 * m))
        q_rows = [slice(c * 2 * m + (0 if rev else m), c * 2 * m + (m if rev else 2 * m)) for c in groups]
        k_cols = [slice(c * 2 * m + (m if rev else 0), c * 2 * m + (2 * m if rev else m)) for c in groups]
        for hd in heads:
            for r in seq_refs:
                qh = r.q[:, _lanes(hd)]
                kh = r.k[hd]
                sel = jnp.concatenate(
                    [(qh if is_q_blk(jb) else kh)[jb * m:(jb + 1) * m] for jb in range(n // m)], axis=0)
                xl = (sel * jnp.exp2(-jnp.abs(r.b[hd] - boundary(r.b, hd, m)))).astype(BF16)
                if m >= BF16_ROWS:
                    lhs = jnp.concatenate([xl[rs] for rs in q_rows], axis=0) if len(q_rows) > 1 \
                        else xl[q_rows[0]]
                    s = _dot_nt(lhs, xl)
                    s_rows = [slice(c * m, (c + 1) * m) for c in groups]
                else:
                    s = _dot_nt(xl, xl)
                    s_rows = q_rows
                for rs, sr, kc in zip(q_rows, s_rows, k_cols):
                    r.p[hd, rs, kc] = s[sr, kc]

    for hd in heads:
        for si, r in enumerate(seq_refs):
            qh = r.q[:, _lanes(hd)]
            b = r.b[hd]
            b_edge = r.b[hd, pl.ds(edge, 1), :]
            qe = (qh * jnp.exp2(b)).astype(BF16)
            ke = (r.k[hd] * jnp.exp2(b_edge - b)).astype(BF16)
            st = r.st[_lanes(hd), :]
            vh = r.v[:, _lanes(hd)]
            o = _dot(r.p[hd].astype(BF16), vh) + _dot_nt(qe, st.astype(BF16))
            if emit is None:
                r.o[:, _lanes(hd)] = o
            else:
                emit(si, hd, o)
            r.st[_lanes(hd), :] = st * jnp.exp2(b_edge) + _dot_tn(vh, ke)


def _hgrn2_scratch(seqs):
    blk = (seqs, N_HEADS, TILE, D_HEAD)
    return [
        pltpu.VMEM(blk, F32),
        pltpu.VMEM(blk, F32),
        pltpu.VMEM(blk, F32),
        pltpu.VMEM((seqs, N_HEADS, TILE, TILE), F32),
    ]


def _const_spec(shape):
    zeros = (0,) * len(shape)
    return pl.BlockSpec(shape, lambda b, j: zeros, pipeline_mode=pl.Buffered(1))


def _fwd_kernel(x_ref, xn_ref, meta_ref, gmix_ref, w_ref, convw_ref, convb_ref, rgw_ref, rgb_ref, lam_ref,
                lbl_ref, mask_ref,
                xc_ref, q_ref, v_ref, hf_ref, of_ref,
                st_scr, hc_scr, xa_scr, a_scr, u_scr, z_scr, k_scr, g_scr, b_scr, p_scr, *, seqs, fused):
    j = pl.program_id(1)
    n_real = pl.num_programs(1) - 1
    is_meta = j == 0
    seq_ids = range(seqs)

    @pl.when(is_meta)
    def _():
        st_scr[...] = jnp.zeros_like(st_scr)
        hc_scr[...] = jnp.zeros_like(hc_scr)
        xa_scr[:, :, 0:HALO, :] = jnp.zeros((seqs, N_HEADS, HALO, D_HEAD), F32)

    mains = [jnp.where(is_meta, meta_ref[...], x_ref[s]) for s in seq_ids]
    halos = [jnp.where(j == n_real, 0.0, xn_ref[s]) for s in seq_ids]
    hx = _rmsnorm(jnp.concatenate(mains + halos, axis=0), gmix_ref[...]).astype(BF16)
    n_main = seqs * TILE
    hx_main = hx[0:n_main, :]

    for c in range(N_HEADS):
        ls = _lanes(c)
        if c % 2 == 0:
            xa = _dot(hx, w_ref[:, c * D_HEAD:(c + 2) * D_HEAD])
        cols = slice((c % 2) * D_HEAD, (c % 2 + 1) * D_HEAD)
        for s in seq_ids:
            xa_scr[s, c, HALO:HALO + TILE, :] = xa[s * TILE:(s + 1) * TILE, cols]
            xa_scr[s, c, HALO + TILE:, :] = xa[n_main + s * HALO:n_main + (s + 1) * HALO, cols]
            acc = convb_ref[:, ls] + convw_ref[0:1, ls] * xa_scr[s, c, pl.ds(HALO - 2, TILE), :]
            for tap in range(1, 4):
                acc = acc + convw_ref[tap:tap + 1, ls] * xa_scr[s, c, pl.ds(HALO - 2 + tap, TILE), :]
            a_scr[s, c] = acc
            xa_scr[s, c, 0:HALO, :] = xa_scr[s, c, TILE:TILE + HALO, :]

    coef = _rglru_coef(lam_ref)
    lb = _lower_bound(lbl_ref[...])
    row = lax.broadcasted_iota(jnp.int32, (TILE, D_HEAD), 0)
    pad = jnp.logical_and(is_meta, row < TILE - N_META)

    def hgrn2_inputs(c, s, zq, zf, zv):
        q_ref[s, :, _lanes(c)] = _silu(zq)
        _forget_gate_block(zf, lb, c, k_scr.at[s], g_scr.at[s], b_scr.at[s])
        v_ref[s, :, _lanes(c)] = zv.astype(BF16)

    for c in range(N_HEADS):
        for s in seq_ids:
            xc = a_scr[s, c]
            xc_ref[s, :, _lanes(c)] = xc
            a, u = _rglru_block(xc, c, rgw_ref, rgb_ref, coef)
            a_scr[s, c] = a
            u_scr[s, c] = jnp.where(pad, 0.0, u)
        if c % 2 == 1:
            i = c // 2
            if fused:
                zq, zf, zv = [_dot(hx_main, w_ref[:, (1 + part) * D_MODEL + i * Z_CHUNK:
                                                     (1 + part) * D_MODEL + (i + 1) * Z_CHUNK]) for part in range(3)]
                for hh in (c - 1, c):
                    cols = slice((hh % 2) * D_HEAD, (hh % 2 + 1) * D_HEAD)
                    for s in seq_ids:
                        rows = slice(s * TILE, (s + 1) * TILE)
                        hgrn2_inputs(hh, s, zq[rows, cols], zf[rows, cols], zv[rows, cols])
            else:
                for part in range(3):
                    lo = part * D_MODEL + i * Z_CHUNK
                    z_scr[:, lo:lo + Z_CHUNK] = _dot(hx_main, w_ref[:, D_MODEL + lo:D_MODEL + lo + Z_CHUNK])
    for s in seq_ids:
        hc_scr[s, 0:1, :] = _scan_linear(a_scr.at[s], u_scr.at[s], hc_scr[s, 0:1, :], rev=False)
    for c in range(N_HEADS):
        for s in seq_ids:
            hf_ref[s, :, _lanes(c)] = u_scr[s, c]

    if not fused:
        for c in range(N_HEADS):
            for s in seq_ids:
                rows = slice(s * TILE, (s + 1) * TILE)
                hgrn2_inputs(c, s, z_scr[rows, _lanes(c)],
                             z_scr[rows, D_MODEL + c * D_HEAD:D_MODEL + (c + 1) * D_HEAD],
                             z_scr[rows, 2 * D_MODEL + c * D_HEAD:2 * D_MODEL + (c + 1) * D_HEAD])

    _hgrn2_tile([_SeqRefs(q_ref.at[s], k_scr.at[s], g_scr.at[s], b_scr.at[s], v_ref.at[s], p_scr.at[s],
                          st_scr.at[s], of_ref.at[s]) for s in seq_ids], mask_ref, rev=False)


def _forward_sweep(x, meta_tile, p, seqs, fused):
    bsz, t, _ = x.shape
    assert bsz % seqs == 0
    nt = t // TILE
    halo_blocks = t // HALO
    tok = pl.BlockSpec((seqs, TILE, D_MODEL), lambda b, j: (b, jnp.maximum(j - 1, 0), 0))
    blk = (seqs, N_HEADS, TILE, D_HEAD)
    out_f32 = jax.ShapeDtypeStruct((bsz, t, D_MODEL), F32)
    return pl.pallas_call(
        functools.partial(_fwd_kernel, seqs=seqs, fused=fused),
        grid=(bsz // seqs, nt + 1),
        in_specs=[
            tok,
            pl.BlockSpec((seqs, HALO, D_MODEL),
                         lambda b, j: (b, jnp.minimum(j * (TILE // HALO), halo_blocks - 1), 0)),
            _const_spec((TILE, D_MODEL)),
            _const_spec((1, D_MODEL)),
            _const_spec((D_MODEL, 4 * D_MODEL)),
            _const_spec((4, D_MODEL)),
            _const_spec((1, D_MODEL)),
            _const_spec((N_HEADS, D_HEAD, 2 * D_HEAD)),
            _const_spec((2, D_MODEL)),
            _const_spec((1, D_MODEL)),
            _const_spec((2, D_MODEL)),
            _const_spec((1 + len(FINE_LEVELS), TILE, TILE)),
        ],
        out_specs=[tok, tok, tok, tok, tok],
        out_shape=[out_f32, out_f32, jax.ShapeDtypeStruct((bsz, t, D_MODEL), BF16), out_f32, out_f32],
        scratch_shapes=[
            pltpu.VMEM((seqs, D_MODEL, D_HEAD), F32),
            pltpu.VMEM((seqs, SUBLANES, D_MODEL), F32),
            pltpu.VMEM((seqs, N_HEADS, TILE + 2 * HALO, D_HEAD), F32),
            pltpu.VMEM(blk, F32),
            pltpu.VMEM(blk, F32),
            pltpu.VMEM((seqs * TILE, 3 * D_MODEL), F32),
        ] + _hgrn2_scratch(seqs),
        compiler_params=pltpu.CompilerParams(
            dimension_semantics=("arbitrary", "arbitrary"), vmem_limit_bytes=VMEM_LIMIT),
        name="fwd_sweep",
    )(x, x, meta_tile, p["gmix"], p["w_fwd"], p["conv_w"], p["conv_b"], p["rgw_f"], p["rgb_f"], p["lam_f"],
      p["lbl_f"], p["mask_f"])


def _bwd_kernel(x_ref, xc_ref, q_ref, v_ref, hf_ref, of_ref, gmix_ref, w_ref, rgw_ref, rgb_ref, lam_ref,
                lbl_ref, mask_ref, hgg_ref, wa_ref, wb_ref, wo_ref,
                y_ref,
                st_scr, hc_scr, z_scr, a_scr, u_scr, ob_scr, ma_scr, ba_scr, bb_scr,
                k_scr, g_scr, b_scr, p_scr, *, seqs, fused):
    j = pl.program_id(1)
    seq_ids = range(seqs)

    @pl.when(j == 0)
    def _():
        st_scr[...] = jnp.zeros_like(st_scr)
        hc_scr[...] = jnp.zeros_like(hc_scr)

    x = jnp.concatenate([x_ref[s] for s in seq_ids], axis=0) if seqs > 1 else x_ref[0]
    hx = _rmsnorm(x, gmix_ref[...]).astype(BF16)

    coef = _rglru_coef(lam_ref)
    for c in range(N_HEADS):
        for s in seq_ids:
            a, u = _rglru_block(xc_ref[s, :, _lanes(c)], c, rgw_ref, rgb_ref, coef)
            a_scr[s, c] = a
            u_scr[s, c] = u
        if c % 2 == 1:
            i = c // 2
            for part in range(5):
                lo = part * D_MODEL + i * Z_CHUNK
                z_scr[:, lo:lo + Z_CHUNK] = _dot(hx, w_ref[:, lo:lo + Z_CHUNK])
    for s in seq_ids:
        hc_scr[s, 0:1, :] = _scan_linear(a_scr.at[s], u_scr.at[s], hc_scr[s, 0:1, :], rev=True)

    lb = _lower_bound(lbl_ref[...])
    for c in range(N_HEADS):
        ls = _lanes(c)
        for s in seq_ids:
            rows = slice(s * TILE, (s + 1) * TILE)
            _forget_gate_block(z_scr[rows, D_MODEL + c * D_HEAD:D_MODEL + (c + 1) * D_HEAD], lb, c,
                               k_scr.at[s], g_scr.at[s], b_scr.at[s])
            ba_scr[rows, ls] = ((hf_ref[s, :, ls] + u_scr[s, c]) * _gelu_tanh(z_scr[rows, ls])).astype(BF16)

    def branch_a_slice(i):
        def run():
            cols = slice(i * Z_CHUNK, (i + 1) * Z_CHUNK)
            gate = _sigmoid(z_scr[:, 3 * D_MODEL + i * Z_CHUNK:3 * D_MODEL + (i + 1) * Z_CHUNK])
            ma_scr[:, cols] = gate * _dot(ba_scr[...], wa_ref[:, cols])
        return run

    def branch_b_head(s, c, o_bwd):
        ls = _lanes(c)
        rows = slice(s * TILE, (s + 1) * TILE)
        o = of_ref[s, :, ls] + o_bwd
        o = o * lax.rsqrt(jnp.mean(o * o, axis=-1, keepdims=True) + EPS)
        og = z_scr[rows, 2 * D_MODEL + c * D_HEAD:2 * D_MODEL + (c + 1) * D_HEAD]
        bb_scr[rows, ls] = (o * hgg_ref[:, ls] * _silu(og)).astype(BF16)

    side_work = [branch_a_slice(i) for i in range(D_MODEL // Z_CHUNK)]
    _hgrn2_tile([_SeqRefs(q_ref.at[s], k_scr.at[s], g_scr.at[s], b_scr.at[s], v_ref.at[s], p_scr.at[s],
                          st_scr.at[s], ob_scr.at[pl.ds(s * TILE, TILE), :]) for s in seq_ids],
                mask_ref, rev=True, side_work=side_work, emit=branch_b_head if fused else None)

    if not fused:
        for c in range(N_HEADS):
            for s in seq_ids:
                branch_b_head(s, c, ob_scr[s * TILE:(s + 1) * TILE, _lanes(c)])

    merged = ma_scr[...] + _sigmoid(z_scr[:, 4 * D_MODEL:5 * D_MODEL]) * _dot(bb_scr[...], wb_ref[...])
    y = x + _dot(merged.astype(BF16), wo_ref[...])
    for s in seq_ids:
        y_ref[s] = y[s * TILE:(s + 1) * TILE, :]


def _backward_sweep(x, xc, q, v, hf, of, p, seqs, fused):
    bsz, t, _ = x.shape
    assert bsz % seqs == 0
    nt = t // TILE
    tok = pl.BlockSpec((seqs, TILE, D_MODEL), lambda b, j: (b, nt - 1 - j, 0))
    blk = (seqs, N_HEADS, TILE, D_HEAD)
    rows = seqs * TILE
    return pl.pallas_call(
        functools.partial(_bwd_kernel, seqs=seqs, fused=fused),
        grid=(bsz // seqs, nt),
        in_specs=[
            tok, tok, tok, tok, tok, tok,
            _const_spec((1, D_MODEL)),
            _const_spec((D_MODEL, 5 * D_MODEL)),
            _const_spec((N_HEADS, D_HEAD, 2 * D_HEAD)),
            _const_spec((2, D_MODEL)),
            _const_spec((1, D_MODEL)),
            _const_spec((2, D_MODEL)),
            _const_spec((1 + len(FINE_LEVELS), TILE, TILE)),
            _const_spec((1, D_MODEL)),
            _const_spec((D_MODEL, D_MODEL)),
            _const_spec((D_MODEL, D_MODEL)),
            _const_spec((D_MODEL, D_MODEL)),
        ],
        out_specs=tok,
        out_shape=jax.ShapeDtypeStruct((bsz, t, D_MODEL), F32),
        scratch_shapes=[
            pltpu.VMEM((seqs, D_MODEL, D_HEAD), F32),
            pltpu.VMEM((seqs, SUBLANES, D_MODEL), F32),
            pltpu.VMEM((rows, 5 * D_MODEL), F32),
            pltpu.VMEM(blk, F32),
            pltpu.VMEM(blk, F32),
            pltpu.VMEM((rows, D_MODEL), F32),
            pltpu.VMEM((rows, D_MODEL), F32),
            pltpu.VMEM((rows, D_MODEL), BF16),
            pltpu.VMEM((rows, D_MODEL), BF16),
        ] + _hgrn2_scratch(seqs),
        compiler_params=pltpu.CompilerParams(
            dimension_semantics=("arbitrary", "arbitrary"), vmem_limit_bytes=VMEM_LIMIT),
        name="bwd_sweep",
    )(x, xc, q, v, hf, of, p["gmix"], p["w_bwd"], p["rgw_b"], p["rgb_b"], p["lam_b"], p["lbl_b"],
      p["mask_b"], p["hgg"], p["wa"], p["wb"], p["wo"])


def _mlp_kernel(x_ref, gmlp_ref, w1_ref, w2_ref, gfin_ref, y_ref):
    x = x_ref[...]
    hx = _rmsnorm(x, gmlp_ref[...]).astype(BF16)
    acc = x
    for c in range(D_FF // FF_CHUNK):
        cols = slice(c * FF_CHUNK, (c + 1) * FF_CHUNK)
        hm = jnp.maximum(_dot(hx, w1_ref[:, cols]), 0.0)
        acc = acc + _dot((hm * hm).astype(BF16), w2_ref[cols, :])
    y_ref[...] = _rmsnorm(acc, gfin_ref[...])


def _channel_mixer(x, p):
    bsz, t, _ = x.shape
    rows = bsz * t
    x2 = x.reshape(rows, D_MODEL)
    tok = pl.BlockSpec((MLP_TILE, D_MODEL), lambda i: (i, 0))
    const2 = lambda i: (0, 0)
    vec = pl.BlockSpec((1, D_MODEL), const2)
    y = pl.pallas_call(
        _mlp_kernel,
        grid=(rows // MLP_TILE,),
        in_specs=[tok, vec, pl.BlockSpec((D_MODEL, D_FF), const2), pl.BlockSpec((D_FF, D_MODEL), const2), vec],
        out_specs=tok,
        out_shape=jax.ShapeDtypeStruct((rows, D_MODEL), F32),
        compiler_params=pltpu.CompilerParams(
            dimension_semantics=("arbitrary",), vmem_limit_bytes=VMEM_LIMIT),
        name="channel_mixer",
    )(x2, p["gmlp"], p["w_mlp1"], p["w_mlp2"], p["gfin"])
    return y.reshape(bsz, t, D_MODEL)


def _prepare_params(meta_tokens, hg_lb_logits, norm_mix_g, w_in, conv_w, conv_b, rg_wa, rg_ba, rg_wx, rg_bx,
                    rg_lambda, hg_norm_g, w_branch_a, w_branch_b, w_out, norm_mlp_g, w_mlp1, w_mlp2,
                    final_norm_g):
    d = D_MODEL
    w = w_in[0].astype(BF16)
    col = lambda i: w[:, i * d:(i + 1) * d]
    row = lambda a: a.reshape(1, d).astype(F32)
    rgw = lambda k: jnp.concatenate([rg_wa[0, k], rg_wx[0, k]], axis=-1).astype(BF16)
    rgb = lambda k: jnp.stack([rg_ba[0, k], rg_bx[0, k]]).astype(F32)
    meta_tile = jnp.concatenate(
        [jnp.zeros((TILE - N_META, d), F32), meta_tokens.astype(F32)], axis=0)
    p = dict(
        gmix=row(norm_mix_g[0]),
        w_fwd=jnp.concatenate([col(0), col(2), col(3), col(5)], axis=1),
        w_bwd=jnp.concatenate([col(1), col(4), col(6), col(7), col(8)], axis=1),
        conv_w=conv_w[0].astype(F32), conv_b=row(conv_b[0]),
        rgw_f=rgw(0), rgw_b=rgw(1), rgb_f=rgb(0), rgb_b=rgb(1),
        lam_f=row(rg_lambda[0, 0]), lam_b=row(rg_lambda[0, 1]),
        lbl_f=hg_lb_logits[:, 0, :].astype(F32), lbl_b=hg_lb_logits[:, 1, :].astype(F32),
        mask_f=jnp.asarray(_fine_masks(TILE, False)), mask_b=jnp.asarray(_fine_masks(TILE, True)),
        hgg=row(hg_norm_g[0]),
        wa=w_branch_a[0].astype(BF16), wb=w_branch_b[0].astype(BF16), wo=w_out[0].astype(BF16),
        gmlp=row(norm_mlp_g[0]), w_mlp1=w_mlp1[0].astype(BF16), w_mlp2=w_mlp2[0].astype(BF16),
        gfin=row(final_norm_g),
    )
    return meta_tile, p


def _encode(x, meta_tile, p, seqs, fused, fwd_seqs=None):
    assert x.shape[1] % TILE == 0 and (x.shape[0] * x.shape[1]) % MLP_TILE == 0
    xc, q, v, hf, of = _forward_sweep(x, meta_tile, p, fwd_seqs or seqs, fused)
    x_mid = _backward_sweep(x, xc, q, v, hf, of, p, seqs, fused)
    return _channel_mixer(x_mid, p)


def kernel(x_prompt, x_sample, meta_tokens, hg_lb_logits, norm_mix_g, w_in, conv_w, conv_b, rg_wa, rg_ba, rg_wx,
           rg_bx, rg_lambda, hg_norm_g, w_branch_a, w_branch_b, w_out, norm_mlp_g, w_mlp1, w_mlp2, final_norm_g):
    meta_tile, p = _prepare_params(meta_tokens, hg_lb_logits, norm_mix_g, w_in, conv_w, conv_b, rg_wa, rg_ba,
                                   rg_wx, rg_bx, rg_lambda, hg_norm_g, w_branch_a, w_branch_b, w_out,
                                   norm_mlp_g, w_mlp1, w_mlp2, final_norm_g)
    return (_encode(x_prompt, meta_tile, p, SEQS_PER_STEP, fused=True),
            _encode(x_sample, meta_tile, p, SEQS_PER_STEP, fused=False, fwd_seqs=4))
```

```python
import functools
from typing import NamedTuple

import numpy as np
import jax
import jax.numpy as jnp
from jax import lax
from jax.experimental import pallas as pl
from jax.experimental.pallas import tpu as pltpu

D_MODEL = 1024
N_META = 16
N_HEADS = 8
D_HEAD = D_MODEL // N_HEADS
D_FF = 4 * D_MODEL
RG_C = 8.0
EPS = 1e-6
LOG2_E = float(np.log2(np.e))

SUBLANES = 8
BF16_ROWS = 16
TILE = 128
FWD_SEQS_PER_STEP = 4
BWD_SEQS_PER_STEP = 2
SEG = TILE // SUBLANES
HALO = SUBLANES
MLP_TILE = 512
FF_CHUNK = 1024
Z_CHUNK = 256
VMEM_LIMIT = 56 * 1024 * 1024

F32 = jnp.float32
BF16 = jnp.bfloat16


def _level_half_sizes(n):
    out, m = [], 1
    while m < n:
        out.append(m)
        m *= 2
    return tuple(out)


LEVELS = _level_half_sizes(TILE)
FINE_LEVELS = tuple(m for m in LEVELS if m < SUBLANES)
COARSE_LEVELS = tuple(m for m in LEVELS if m >= SUBLANES)


def _fine_masks(n, rev):
    t = np.arange(n)[:, None]
    s = np.arange(n)[None, :]
    masks = [t == s]
    for m in FINE_LEVELS:
        same = (t // (2 * m)) == (s // (2 * m))
        t_up = (t % (2 * m)) >= m
        s_up = (s % (2 * m)) >= m
        masks.append(same & (~t_up) & s_up if rev else same & t_up & (~s_up))
    return np.stack(masks).astype(np.float32)


def _rmsnorm(x, g):
    ms = jnp.mean(x * x, axis=-1, keepdims=True)
    return x * lax.rsqrt(ms + EPS) * g


def _dot(a, b):
    return jnp.dot(a, b, preferred_element_type=F32)


def _dot_nt(a, b):
    return lax.dot_general(a, b, (((1,), (1,)), ((), ())), preferred_element_type=F32)


def _dot_tn(a, b):
    return lax.dot_general(a, b, (((0,), (0,)), ((), ())), preferred_element_type=F32)


def _gelu_tanh(x):
    c = np.float32(np.sqrt(2.0 / np.pi))
    return 0.5 * x * (1.0 + jnp.tanh(c * (x + 0.044715 * (x * x * x))))


def _sigmoid(x):
    return 0.5 * jnp.tanh(0.5 * x) + 0.5


def _silu(x):
    hx = 0.5 * x
    return hx * jnp.tanh(hx) + hx


def _lower_bound(lbl):
    mx = jnp.maximum(lbl[0:1, :], lbl[1:2, :])
    e0 = jnp.exp(lbl[0:1, :] - mx)
    e1 = jnp.exp(lbl[1:2, :] - mx)
    return e0 / (e0 + e1)


def _lanes(c):
    return slice(c * D_HEAD, (c + 1) * D_HEAD)


def _seg_order(n, rev):
    return range(n - 1, -1, -1) if rev else range(n)


def _scan_linear(a_ref, u_ref, carry, rev):
    hs = [jnp.zeros((SUBLANES, D_HEAD), F32)] * N_HEADS
    ps = [jnp.ones((SUBLANES, D_HEAD), F32)] * N_HEADS
    for gi in _seg_order(SEG, rev):
        idx = pl.ds(gi, SUBLANES, stride=SEG)
        for c in range(N_HEADS):
            a = a_ref[c, idx, :]
            hs[c] = a * hs[c] + u_ref[c, idx, :]
            ps[c] = a * ps[c]
            u_ref[c, idx, :] = hs[c]
            a_ref[c, idx, :] = ps[c]
    cs = [carry[:, _lanes(c)] for c in range(N_HEADS)]
    for r in _seg_order(SUBLANES, rev):
        rows = pl.ds(r * SEG, SEG)
        for c in range(N_HEADS):
            u_ref[c, rows, :] = u_ref[c, rows, :] + a_ref[c, rows, :] * cs[c]
            cs[c] = ps[c][r:r + 1, :] * cs[c] + hs[c][r:r + 1, :]
    return jnp.concatenate(cs, axis=1)


def _scan_sum(g_ref, rev):
    hs = [jnp.zeros((SUBLANES, D_HEAD), F32)] * N_HEADS
    for gi in _seg_order(SEG, rev):
        idx = pl.ds(gi, SUBLANES, stride=SEG)
        for c in range(N_HEADS):
            hs[c] = hs[c] + g_ref[c, idx, :]
            g_ref[c, idx, :] = hs[c]
    cs = [jnp.zeros((1, D_HEAD), F32)] * N_HEADS
    for r in _seg_order(SUBLANES, rev):
        rows = pl.ds(r * SEG, SEG)
        for c in range(N_HEADS):
            g_ref[c, rows, :] = g_ref[c, rows, :] + cs[c]
            cs[c] = cs[c] + hs[c][r:r + 1, :]


def _rglru_coef(lam_ref):
    nl = -lam_ref[...]
    return -RG_C * (jnp.maximum(nl, 0.0) + jnp.log1p(jnp.exp(-jnp.abs(nl))))


def _rglru_block(xc, c, rgw_ref, rgb_ref, coef):
    pre = _dot(xc.astype(BF16), rgw_ref[c])
    r = _sigmoid(pre[:, :D_HEAD] + rgb_ref[0:1, _lanes(c)])
    i = _sigmoid(pre[:, D_HEAD:] + rgb_ref[1:2, _lanes(c)])
    log_a = coef[:, _lanes(c)] * r
    a = jnp.exp(log_a)
    th = jnp.tanh(log_a)
    sq = -2.0 * th / (1.0 - th)
    mult = jnp.where(sq > 0.0, sq * lax.rsqrt(sq), 0.0)
    return a, mult * (i * xc)


def _forget_gate_block(ff, lb, c, k_ref, g_ref, b_ref):
    f = lb[:, _lanes(c)] + (1.0 - lb[:, _lanes(c)]) * _sigmoid(ff)
    g = jnp.log(f) * LOG2_E
    k_ref[c] = 1.0 - f
    g_ref[c] = g
    b_ref[c] = g


class _SeqRefs(NamedTuple):
    q: object
    k: object
    g: object
    b: object
    v: object
    p: object
    st: object
    o: object


def _hgrn2_tile(seq_refs, mask_ref, rev, side_work=()):
    n = TILE
    heads = range(N_HEADS)
    edge = 0 if rev else n - 1
    sub = lax.broadcasted_iota(jnp.int32, (n, D_HEAD), 0) % SUBLANES

    def boundary(b_ref, hd, m):
        parts = []
        for c in range(n // (2 * m)):
            r0 = c * 2 * m + (m if rev else m - 1)
            parts.append(jnp.broadcast_to(b_ref[hd, pl.ds(r0, 1), :], (2 * m, D_HEAD)))
        return jnp.concatenate(parts, axis=0) if len(parts) > 1 else parts[0]

    for r in seq_refs:
        _scan_sum(r.b, rev)

    for hd in heads:
        for r in seq_refs:
            qh = r.q[:, _lanes(hd)]
            kh = r.k[hd]
            gh = r.g[hd]
            acc = mask_ref[0] * _dot_nt(qh.astype(BF16), kh.astype(BF16))
            for li, m in enumerate(FINE_LEVELS):
                pos = sub % (2 * m)
                is_q = (pos < m) if rev else (pos >= m)
                if m == 1:
                    e = jnp.where(is_q, gh, 0.0)
                elif m == 2:
                    g_nxt = pltpu.roll(gh, n - 1, axis=0)
                    g_prv = pltpu.roll(gh, 1, axis=0)
                    if rev:
                        e = jnp.where(pos == 0, gh + g_nxt,
                                      jnp.where(pos == 1, gh, jnp.where(pos == 2, 0.0, g_prv)))
                    else:
                        e = jnp.where(pos == 0, g_nxt,
                                      jnp.where(pos == 1, 0.0, jnp.where(pos == 2, gh, gh + g_prv)))
                else:
                    e = -jnp.abs(r.b[hd] - boundary(r.b, hd, m))
                xl = (jnp.where(is_q, qh, kh) * jnp.exp2(e)).astype(BF16)
                acc = acc + mask_ref[1 + li] * _dot_nt(xl, xl)
            r.p[hd] = acc

    for ci, m in enumerate(COARSE_LEVELS):
        if ci < len(side_work):
            side_work[ci]()
        is_q_blk = lambda jb: (jb % 2 == 1) != rev
        groups = range(n // (2 * m))
        q_rows = [slice(c * 2 * m + (0 if rev else m), c * 2 * m + (m if rev else 2 * m)) for c in groups]
        k_cols = [slice(c * 2 * m + (m if rev else 0), c * 2 * m + (2 * m if rev else m)) for c in groups]
        for hd in heads:
            for r in seq_refs:
                qh = r.q[:, _lanes(hd)]
                kh = r.k[hd]
                sel = jnp.concatenate(
                    [(qh if is_q_blk(jb) else kh)[jb * m:(jb + 1) * m] for jb in range(n // m)], axis=0)
                xl = (sel * jnp.exp2(-jnp.abs(r.b[hd] - boundary(r.b, hd, m)))).astype(BF16)
                if m >= BF16_ROWS:
                    lhs = jnp.concatenate([xl[rs] for rs in q_rows], axis=0) if len(q_rows) > 1 \
                        else xl[q_rows[0]]
                    s = _dot_nt(lhs, xl)
                    s_rows = [slice(c * m, (c + 1) * m) for c in groups]
                else:
                    s = _dot_nt(xl, xl)
                    s_rows = q_rows
                for rs, sr, kc in zip(q_rows, s_rows, k_cols):
                    r.p[hd, rs, kc] = s[sr, kc]

    for hd in heads:
        for r in seq_refs:
            qh = r.q[:, _lanes(hd)]
            b = r.b[hd]
            b_edge = r.b[hd, pl.ds(edge, 1), :]
            qe = (qh * jnp.exp2(b)).astype(BF16)
            ke = (r.k[hd] * jnp.exp2(b_edge - b)).astype(BF16)
            st = r.st[_lanes(hd), :]
            vh = r.v[:, _lanes(hd)]
            r.o[:, _lanes(hd)] = _dot(r.p[hd].astype(BF16), vh) + _dot_nt(qe, st.astype(BF16))
            r.st[_lanes(hd), :] = st * jnp.exp2(b_edge) + _dot_tn(vh, ke)


def _hgrn2_scratch(seqs):
    blk = (seqs, N_HEADS, TILE, D_HEAD)
    return [
        pltpu.VMEM(blk, F32),
        pltpu.VMEM(blk, F32),
        pltpu.VMEM(blk, F32),
        pltpu.VMEM((seqs, N_HEADS, TILE, TILE), F32),
    ]


def _const_spec(shape):
    zeros = (0,) * len(shape)
    return pl.BlockSpec(shape, lambda b, j: zeros, pipeline_mode=pl.Buffered(1))


def _fwd_kernel(x_ref, xn_ref, meta_ref, gmix_ref, w_ref, convw_ref, convb_ref, rgw_ref, rgb_ref, lam_ref,
                lbl_ref, mask_ref,
                xc_ref, q_ref, v_ref, hf_ref, of_ref,
                st_scr, hc_scr, xa_scr, a_scr, u_scr, z_scr, k_scr, g_scr, b_scr, p_scr, *, seqs):
    j = pl.program_id(1)
    n_real = pl.num_programs(1) - 1
    is_meta = j == 0
    seq_ids = range(seqs)

    @pl.when(is_meta)
    def _():
        st_scr[...] = jnp.zeros_like(st_scr)
        hc_scr[...] = jnp.zeros_like(hc_scr)
        xa_scr[:, :, 0:HALO, :] = jnp.zeros((seqs, N_HEADS, HALO, D_HEAD), F32)

    mains = [jnp.where(is_meta, meta_ref[...], x_ref[s]) for s in seq_ids]
    halos = [jnp.where(j == n_real, 0.0, xn_ref[s]) for s in seq_ids]
    hx = _rmsnorm(jnp.concatenate(mains + halos, axis=0), gmix_ref[...]).astype(BF16)
    n_main = seqs * TILE
    hx_main = hx[0:n_main, :]

    for c in range(N_HEADS):
        ls = _lanes(c)
        if c % 2 == 0:
            xa = _dot(hx, w_ref[:, c * D_HEAD:(c + 2) * D_HEAD])
        cols = slice((c % 2) * D_HEAD, (c % 2 + 1) * D_HEAD)
        for s in seq_ids:
            xa_scr[s, c, HALO:HALO + TILE, :] = xa[s * TILE:(s + 1) * TILE, cols]
            xa_scr[s, c, HALO + TILE:, :] = xa[n_main + s * HALO:n_main + (s + 1) * HALO, cols]
            acc = convb_ref[:, ls] + convw_ref[0:1, ls] * xa_scr[s, c, pl.ds(HALO - 2, TILE), :]
            for tap in range(1, 4):
                acc = acc + convw_ref[tap:tap + 1, ls] * xa_scr[s, c, pl.ds(HALO - 2 + tap, TILE), :]
            a_scr[s, c] = acc
            xa_scr[s, c, 0:HALO, :] = xa_scr[s, c, TILE:TILE + HALO, :]

    coef = _rglru_coef(lam_ref)
    lb = _lower_bound(lbl_ref[...])
    row = lax.broadcasted_iota(jnp.int32, (TILE, D_HEAD), 0)
    pad = jnp.logical_and(is_meta, row < TILE - N_META)

    for c in range(N_HEADS):
        for s in seq_ids:
            xc = a_scr[s, c]
            xc_ref[s, :, _lanes(c)] = xc
            a, u = _rglru_block(xc, c, rgw_ref, rgb_ref, coef)
            a_scr[s, c] = a
            u_scr[s, c] = jnp.where(pad, 0.0, u)
        if c % 2 == 1:
            i = c // 2
            for part in range(3):
                lo = part * D_MODEL + i * Z_CHUNK
                z_scr[:, lo:lo + Z_CHUNK] = _dot(hx_main, w_ref[:, D_MODEL + lo:D_MODEL + lo + Z_CHUNK])
    for s in seq_ids:
        hc_scr[s, 0:1, :] = _scan_linear(a_scr.at[s], u_scr.at[s], hc_scr[s, 0:1, :], rev=False)
    for c in range(N_HEADS):
        for s in seq_ids:
            hf_ref[s, :, _lanes(c)] = u_scr[s, c]

    for c in range(N_HEADS):
        for s in seq_ids:
            rows = slice(s * TILE, (s + 1) * TILE)
            q_ref[s, :, _lanes(c)] = _silu(z_scr[rows, _lanes(c)])
            _forget_gate_block(z_scr[rows, D_MODEL + c * D_HEAD:D_MODEL + (c + 1) * D_HEAD], lb, c,
                               k_scr.at[s], g_scr.at[s], b_scr.at[s])
            v_ref[s, :, _lanes(c)] = z_scr[rows, 2 * D_MODEL + c * D_HEAD:2 * D_MODEL + (c + 1) * D_HEAD].astype(BF16)

    _hgrn2_tile([_SeqRefs(q_ref.at[s], k_scr.at[s], g_scr.at[s], b_scr.at[s], v_ref.at[s], p_scr.at[s],
                          st_scr.at[s], of_ref.at[s]) for s in seq_ids], mask_ref, rev=False)


def _forward_sweep(x, meta_tile, p, seqs):
    bsz, t, _ = x.shape
    assert bsz % seqs == 0
    nt = t // TILE
    halo_blocks = t // HALO
    tok = pl.BlockSpec((seqs, TILE, D_MODEL), lambda b, j: (b, jnp.maximum(j - 1, 0), 0))
    blk = (seqs, N_HEADS, TILE, D_HEAD)
    out_f32 = jax.ShapeDtypeStruct((bsz, t, D_MODEL), F32)
    return pl.pallas_call(
        functools.partial(_fwd_kernel, seqs=seqs),
        grid=(bsz // seqs, nt + 1),
        in_specs=[
            tok,
            pl.BlockSpec((seqs, HALO, D_MODEL),
                         lambda b, j: (b, jnp.minimum(j * (TILE // HALO), halo_blocks - 1), 0)),
            _const_spec((TILE, D_MODEL)),
            _const_spec((1, D_MODEL)),
            _const_spec((D_MODEL, 4 * D_MODEL)),
            _const_spec((4, D_MODEL)),
            _const_spec((1, D_MODEL)),
            _const_spec((N_HEADS, D_HEAD, 2 * D_HEAD)),
            _const_spec((2, D_MODEL)),
            _const_spec((1, D_MODEL)),
            _const_spec((2, D_MODEL)),
            _const_spec((1 + len(FINE_LEVELS), TILE, TILE)),
        ],
        out_specs=[tok, tok, tok, tok, tok],
        out_shape=[out_f32, out_f32, jax.ShapeDtypeStruct((bsz, t, D_MODEL), BF16), out_f32, out_f32],
        scratch_shapes=[
            pltpu.VMEM((seqs, D_MODEL, D_HEAD), F32),
            pltpu.VMEM((seqs, SUBLANES, D_MODEL), F32),
            pltpu.VMEM((seqs, N_HEADS, TILE + 2 * HALO, D_HEAD), F32),
            pltpu.VMEM(blk, F32),
            pltpu.VMEM(blk, F32),
            pltpu.VMEM((seqs * TILE, 3 * D_MODEL), F32),
        ] + _hgrn2_scratch(seqs),
        compiler_params=pltpu.CompilerParams(
            dimension_semantics=("arbitrary", "arbitrary"), vmem_limit_bytes=VMEM_LIMIT),
        name="fwd_sweep",
    )(x, x, meta_tile, p["gmix"], p["w_fwd"], p["conv_w"], p["conv_b"], p["rgw_f"], p["rgb_f"], p["lam_f"],
      p["lbl_f"], p["mask_f"])


def _bwd_kernel(x_ref, xc_ref, q_ref, v_ref, hf_ref, of_ref, gmix_ref, w_ref, rgw_ref, rgb_ref, lam_ref,
                lbl_ref, mask_ref, hgg_ref, wa_ref, wb_ref, wo_ref,
                y_ref,
                st_scr, hc_scr, z_scr, a_scr, u_scr, ob_scr, ma_scr, ba_scr, bb_scr,
                k_scr, g_scr, b_scr, p_scr, *, seqs):
    j = pl.program_id(1)
    seq_ids = range(seqs)

    @pl.when(j == 0)
    def _():
        st_scr[...] = jnp.zeros_like(st_scr)
        hc_scr[...] = jnp.zeros_like(hc_scr)

    x = jnp.concatenate([x_ref[s] for s in seq_ids], axis=0) if seqs > 1 else x_ref[0]
    hx = _rmsnorm(x, gmix_ref[...]).astype(BF16)

    coef = _rglru_coef(lam_ref)
    for c in range(N_HEADS):
        for s in seq_ids:
            a, u = _rglru_block(xc_ref[s, :, _lanes(c)], c, rgw_ref, rgb_ref, coef)
            a_scr[s, c] = a
            u_scr[s, c] = u
        if c % 2 == 1:
            i = c // 2
            for part in range(5):
                lo = part * D_MODEL + i * Z_CHUNK
                z_scr[:, lo:lo + Z_CHUNK] = _dot(hx, w_ref[:, lo:lo + Z_CHUNK])
    for s in seq_ids:
        hc_scr[s, 0:1, :] = _scan_linear(a_scr.at[s], u_scr.at[s], hc_scr[s, 0:1, :], rev=True)

    lb = _lower_bound(lbl_ref[...])
    for c in range(N_HEADS):
        ls = _lanes(c)
        for s in seq_ids:
            rows = slice(s * TILE, (s + 1) * TILE)
            _forget_gate_block(z_scr[rows, D_MODEL + c * D_HEAD:D_MODEL + (c + 1) * D_HEAD], lb, c,
                               k_scr.at[s], g_scr.at[s], b_scr.at[s])
            ba_scr[rows, ls] = ((hf_ref[s, :, ls] + u_scr[s, c]) * _gelu_tanh(z_scr[rows, ls])).astype(BF16)

    def branch_a_slice(i):
        def run():
            cols = slice(i * Z_CHUNK, (i + 1) * Z_CHUNK)
            gate = _sigmoid(z_scr[:, 3 * D_MODEL + i * Z_CHUNK:3 * D_MODEL + (i + 1) * Z_CHUNK])
            ma_scr[:, cols] = gate * _dot(ba_scr[...], wa_ref[:, cols])
        return run

    side_work = [branch_a_slice(i) for i in range(D_MODEL // Z_CHUNK)]
    _hgrn2_tile([_SeqRefs(q_ref.at[s], k_scr.at[s], g_scr.at[s], b_scr.at[s], v_ref.at[s], p_scr.at[s],
                          st_scr.at[s], ob_scr.at[pl.ds(s * TILE, TILE), :]) for s in seq_ids],
                mask_ref, rev=True, side_work=side_work)

    for c in range(N_HEADS):
        ls = _lanes(c)
        for s in seq_ids:
            rows = slice(s * TILE, (s + 1) * TILE)
            o = of_ref[s, :, ls] + ob_scr[rows, ls]
            o = o * lax.rsqrt(jnp.mean(o * o, axis=-1, keepdims=True) + EPS)
            og = z_scr[rows, 2 * D_MODEL + c * D_HEAD:2 * D_MODEL + (c + 1) * D_HEAD]
            bb_scr[rows, ls] = (o * hgg_ref[:, ls] * _silu(og)).astype(BF16)

    merged = ma_scr[...] + _sigmoid(z_scr[:, 4 * D_MODEL:5 * D_MODEL]) * _dot(bb_scr[...], wb_ref[...])
    y = x + _dot(merged.astype(BF16), wo_ref[...])
    for s in seq_ids:
        y_ref[s] = y[s * TILE:(s + 1) * TILE, :]


def _backward_sweep(x, xc, q, v, hf, of, p, seqs):
    bsz, t, _ = x.shape
    assert bsz % seqs == 0
    nt = t // TILE
    tok = pl.BlockSpec((seqs, TILE, D_MODEL), lambda b, j: (b, nt - 1 - j, 0))
    blk = (seqs, N_HEADS, TILE, D_HEAD)
    rows = seqs * TILE
    return pl.pallas_call(
        functools.partial(_bwd_kernel, seqs=seqs),
        grid=(bsz // seqs, nt),
        in_specs=[
            tok, tok, tok, tok, tok, tok,
            _const_spec((1, D_MODEL)),
            _const_spec((D_MODEL, 5 * D_MODEL)),
            _const_spec((N_HEADS, D_HEAD, 2 * D_HEAD)),
            _const_spec((2, D_MODEL)),
            _const_spec((1, D_MODEL)),
            _const_spec((2, D_MODEL)),
            _const_spec((1 + len(FINE_LEVELS), TILE, TILE)),
            _const_spec((1, D_MODEL)),
            _const_spec((D_MODEL, D_MODEL)),
            _const_spec((D_MODEL, D_MODEL)),
            _const_spec((D_MODEL, D_MODEL)),
        ],
        out_specs=tok,
        out_shape=jax.ShapeDtypeStruct((bsz, t, D_MODEL), F32),
        scratch_shapes=[
            pltpu.VMEM((seqs, D_MODEL, D_HEAD), F32),
            pltpu.VMEM((seqs, SUBLANES, D_MODEL), F32),
            pltpu.VMEM((rows, 5 * D_MODEL), F32),
            pltpu.VMEM(blk, F32),
            pltpu.VMEM(blk, F32),
            pltpu.VMEM((rows, D_MODEL), F32),
            pltpu.VMEM((rows, D_MODEL), F32),
            pltpu.VMEM((rows, D_MODEL), BF16),
            pltpu.VMEM((rows, D_MODEL), BF16),
        ] + _hgrn2_scratch(seqs),
        compiler_params=pltpu.CompilerParams(
            dimension_semantics=("arbitrary", "arbitrary"), vmem_limit_bytes=VMEM_LIMIT),
        name="bwd_sweep",
    )(x, xc, q, v, hf, of, p["gmix"], p["w_bwd"], p["rgw_b"], p["rgb_b"], p["lam_b"], p["lbl_b"],
      p["mask_b"], p["hgg"], p["wa"], p["wb"], p["wo"])


def _mlp_kernel(x_ref, gmlp_ref, w1_ref, w2_ref, gfin_ref, y_ref):
    x = x_ref[...]
    hx = _rmsnorm(x, gmlp_ref[...]).astype(BF16)
    acc = x
    for c in range(D_FF // FF_CHUNK):
        cols = slice(c * FF_CHUNK, (c + 1) * FF_CHUNK)
        hm = jnp.maximum(_dot(hx, w1_ref[:, cols]), 0.0)
        acc = acc + _dot((hm * hm).astype(BF16), w2_ref[cols, :])
    y_ref[...] = _rmsnorm(acc, gfin_ref[...])


def _channel_mixer(x, p):
    bsz, t, _ = x.shape
    rows = bsz * t
    x2 = x.reshape(rows, D_MODEL)
    tok = pl.BlockSpec((MLP_TILE, D_MODEL), lambda i: (i, 0))
    const2 = lambda i: (0, 0)
    vec = pl.BlockSpec((1, D_MODEL), const2)
    y = pl.pallas_call(
        _mlp_kernel,
        grid=(rows // MLP_TILE,),
        in_specs=[tok, vec, pl.BlockSpec((D_MODEL, D_FF), const2), pl.BlockSpec((D_FF, D_MODEL), const2), vec],
        out_specs=tok,
        out_shape=jax.ShapeDtypeStruct((rows, D_MODEL), F32),
        compiler_params=pltpu.CompilerParams(
            dimension_semantics=("arbitrary",), vmem_limit_bytes=VMEM_LIMIT),
        name="channel_mixer",
    )(x2, p["gmlp"], p["w_mlp1"], p["w_mlp2"], p["gfin"])
    return y.reshape(bsz, t, D_MODEL)


def _prepare_params(meta_tokens, hg_lb_logits, norm_mix_g, w_in, conv_w, conv_b, rg_wa, rg_ba, rg_wx, rg_bx,
                    rg_lambda, hg_norm_g, w_branch_a, w_branch_b, w_out, norm_mlp_g, w_mlp1, w_mlp2,
                    final_norm_g):
    d = D_MODEL
    w = w_in[0].astype(BF16)
    col = lambda i: w[:, i * d:(i + 1) * d]
    row = lambda a: a.reshape(1, d).astype(F32)
    rgw = lambda k: jnp.concatenate([rg_wa[0, k], rg_wx[0, k]], axis=-1).astype(BF16)
    rgb = lambda k: jnp.stack([rg_ba[0, k], rg_bx[0, k]]).astype(F32)
    meta_tile = jnp.concatenate(
        [jnp.zeros((TILE - N_META, d), F32), meta_tokens.astype(F32)], axis=0)
    p = dict(
        gmix=row(norm_mix_g[0]),
        w_fwd=jnp.concatenate([col(0), col(2), col(3), col(5)], axis=1),
        w_bwd=jnp.concatenate([col(1), col(4), col(6), col(7), col(8)], axis=1),
        conv_w=conv_w[0].astype(F32), conv_b=row(conv_b[0]),
        rgw_f=rgw(0), rgw_b=rgw(1), rgb_f=rgb(0), rgb_b=rgb(1),
        lam_f=row(rg_lambda[0, 0]), lam_b=row(rg_lambda[0, 1]),
        lbl_f=hg_lb_logits[:, 0, :].astype(F32), lbl_b=hg_lb_logits[:, 1, :].astype(F32),
        mask_f=jnp.asarray(_fine_masks(TILE, False)), mask_b=jnp.asarray(_fine_masks(TILE, True)),
        hgg=row(hg_norm_g[0]),
        wa=w_branch_a[0].astype(BF16), wb=w_branch_b[0].astype(BF16), wo=w_out[0].astype(BF16),
        gmlp=row(norm_mlp_g[0]), w_mlp1=w_mlp1[0].astype(BF16), w_mlp2=w_mlp2[0].astype(BF16),
        gfin=row(final_norm_g),
    )
    return meta_tile, p


def _seqs_per_step(bsz, cap):
    return max(n for n in range(1, cap + 1) if bsz % n == 0)


def _encode(x, meta_tile, p):
    bsz, t, _ = x.shape
    assert t % TILE == 0 and (bsz * t) % MLP_TILE == 0
    xc, q, v, hf, of = _forward_sweep(x, meta_tile, p, _seqs_per_step(bsz, FWD_SEQS_PER_STEP))
    x_mid = _backward_sweep(x, xc, q, v, hf, of, p, _seqs_per_step(bsz, BWD_SEQS_PER_STEP))
    return _channel_mixer(x_mid, p)


def kernel(x_prompt, x_sample, meta_tokens, hg_lb_logits, norm_mix_g, w_in, conv_w, conv_b, rg_wa, rg_ba, rg_wx,
           rg_bx, rg_lambda, hg_norm_g, w_branch_a, w_branch_b, w_out, norm_mlp_g, w_mlp1, w_mlp2, final_norm_g):
    meta_tile, p = _prepare_params(meta_tokens, hg_lb_logits, norm_mix_g, w_in, conv_w, conv_b, rg_wa, rg_ba,
                                   rg_wx, rg_bx, rg_lambda, hg_norm_g, w_branch_a, w_branch_b, w_out,
                                   norm_mlp_g, w_mlp1, w_mlp2, final_norm_g)
    return (_encode(x_prompt, meta_tile, p), _encode(x_sample, meta_tile, p))
```

```python
import functools
from typing import NamedTuple

import numpy as np
import jax
import jax.numpy as jnp
from jax import lax
from jax.experimental import pallas as pl
from jax.experimental.pallas import tpu as pltpu

D_MODEL = 1024
N_META = 16
N_HEADS = 8
D_HEAD = D_MODEL // N_HEADS
D_FF = 4 * D_MODEL
RG_C = 8.0
EPS = 1e-6
LOG2_E = float(np.log2(np.e))

SUBLANES = 8
BF16_ROWS = 16
TILE = 128
FWD_SEQS_PER_STEP = 4
BWD_SEQS_PER_STEP = 2
SEG = TILE // SUBLANES
HALO = SUBLANES
MLP_TILE = 1024
FF_CHUNK = 1024
Z_CHUNK = 256
VMEM_LIMIT = 56 * 1024 * 1024

F32 = jnp.float32
BF16 = jnp.bfloat16


def _level_half_sizes(n):
    out, m = [], 1
    while m < n:
        out.append(m)
        m *= 2
    return tuple(out)


LEVELS = _level_half_sizes(TILE)
FINE_LEVELS = tuple(m for m in LEVELS if m < SUBLANES)
COARSE_LEVELS = tuple(m for m in LEVELS if m >= SUBLANES)


def _fine_masks(n, rev):
    t = np.arange(n)[:, None]
    s = np.arange(n)[None, :]
    masks = [t == s]
    for m in FINE_LEVELS:
        same = (t // (2 * m)) == (s // (2 * m))
        t_up = (t % (2 * m)) >= m
        s_up = (s % (2 * m)) >= m
        masks.append(same & (~t_up) & s_up if rev else same & t_up & (~s_up))
    return np.stack(masks).astype(np.float32)


def _rmsnorm(x, g):
    ms = jnp.mean(x * x, axis=-1, keepdims=True)
    return x * lax.rsqrt(ms + EPS) * g


def _dot(a, b):
    return jnp.dot(a, b, preferred_element_type=F32)


def _dot_nt(a, b):
    return lax.dot_general(a, b, (((1,), (1,)), ((), ())), preferred_element_type=F32)


def _dot_tn(a, b):
    return lax.dot_general(a, b, (((0,), (0,)), ((), ())), preferred_element_type=F32)


def _gelu_tanh(x):
    c = np.float32(np.sqrt(2.0 / np.pi))
    return 0.5 * x * (1.0 + jnp.tanh(c * (x + 0.044715 * (x * x * x))))


def _sigmoid(x):
    return 0.5 * jnp.tanh(0.5 * x) + 0.5


def _silu(x):
    hx = 0.5 * x
    return hx * jnp.tanh(hx) + hx


def _lower_bound(lbl):
    mx = jnp.maximum(lbl[0:1, :], lbl[1:2, :])
    e0 = jnp.exp(lbl[0:1, :] - mx)
    e1 = jnp.exp(lbl[1:2, :] - mx)
    return e0 / (e0 + e1)


def _lanes(c):
    return slice(c * D_HEAD, (c + 1) * D_HEAD)


def _seg_order(n, rev):
    return range(n - 1, -1, -1) if rev else range(n)


def _scan_linear(a_ref, u_ref, carry, rev):
    hs = [jnp.zeros((SUBLANES, D_HEAD), F32)] * N_HEADS
    ps = [jnp.ones((SUBLANES, D_HEAD), F32)] * N_HEADS
    for gi in _seg_order(SEG, rev):
        idx = pl.ds(gi, SUBLANES, stride=SEG)
        for c in range(N_HEADS):
            a = a_ref[c, idx, :]
            hs[c] = a * hs[c] + u_ref[c, idx, :]
            ps[c] = a * ps[c]
            u_ref[c, idx, :] = hs[c]
            a_ref[c, idx, :] = ps[c]
    cs = [carry[:, _lanes(c)] for c in range(N_HEADS)]
    for r in _seg_order(SUBLANES, rev):
        rows = pl.ds(r * SEG, SEG)
        for c in range(N_HEADS):
            u_ref[c, rows, :] = u_ref[c, rows, :] + a_ref[c, rows, :] * cs[c]
            cs[c] = ps[c][r:r + 1, :] * cs[c] + hs[c][r:r + 1, :]
    return jnp.concatenate(cs, axis=1)


def _scan_sum(g_ref, rev):
    hs = [jnp.zeros((SUBLANES, D_HEAD), F32)] * N_HEADS
    for gi in _seg_order(SEG, rev):
        idx = pl.ds(gi, SUBLANES, stride=SEG)
        for c in range(N_HEADS):
            hs[c] = hs[c] + g_ref[c, idx, :]
            g_ref[c, idx, :] = hs[c]
    cs = [jnp.zeros((1, D_HEAD), F32)] * N_HEADS
    for r in _seg_order(SUBLANES, rev):
        rows = pl.ds(r * SEG, SEG)
        for c in range(N_HEADS):
            g_ref[c, rows, :] = g_ref[c, rows, :] + cs[c]
            cs[c] = cs[c] + hs[c][r:r + 1, :]


def _rglru_coef(lam_ref):
    nl = -lam_ref[...]
    return -RG_C * (jnp.maximum(nl, 0.0) + jnp.log1p(jnp.exp(-jnp.abs(nl))))


def _rglru_block(xc, c, rgw_ref, rgb_ref, coef):
    pre = _dot(xc.astype(BF16), rgw_ref[c])
    r = _sigmoid(pre[:, :D_HEAD] + rgb_ref[0:1, _lanes(c)])
    i = _sigmoid(pre[:, D_HEAD:] + rgb_ref[1:2, _lanes(c)])
    log_a = coef[:, _lanes(c)] * r
    a = jnp.exp(log_a)
    th = jnp.tanh(log_a)
    sq = -2.0 * th / (1.0 - th)
    mult = jnp.where(sq > 0.0, sq * lax.rsqrt(sq), 0.0)
    return a, mult * (i * xc)


def _forget_gate_block(ff, lb, c, k_ref, g_ref, b_ref):
    f = lb[:, _lanes(c)] + (1.0 - lb[:, _lanes(c)]) * _sigmoid(ff)
    g = jnp.log(f) * LOG2_E
    k_ref[c] = 1.0 - f
    g_ref[c] = g
    b_ref[c] = g


class _SeqRefs(NamedTuple):
    q: object
    k: object
    g: object
    b: object
    v: object
    p: object
    st: object
    o: object


def _hgrn2_tile(seq_refs, mask_ref, rev, side_work=()):
    n = TILE
    heads = range(N_HEADS)
    edge = 0 if rev else n - 1
    sub = lax.broadcasted_iota(jnp.int32, (n, D_HEAD), 0) % SUBLANES

    def boundary(b_ref, hd, m):
        parts = []
        for c in range(n // (2 * m)):
            r0 = c * 2 * m + (m if rev else m - 1)
            parts.append(jnp.broadcast_to(b_ref[hd, pl.ds(r0, 1), :], (2 * m, D_HEAD)))
        return jnp.concatenate(parts, axis=0) if len(parts) > 1 else parts[0]

    for r in seq_refs:
        _scan_sum(r.b, rev)

    for hd in heads:
        for r in seq_refs:
            qh = r.q[:, _lanes(hd)]
            kh = r.k[hd]
            gh = r.g[hd]
            acc = mask_ref[0] * _dot_nt(qh.astype(BF16), kh.astype(BF16))
            for li, m in enumerate(FINE_LEVELS):
                pos = sub % (2 * m)
                is_q = (pos < m) if rev else (pos >= m)
                if m == 1:
                    e = jnp.where(is_q, gh, 0.0)
                elif m == 2:
                    g_nxt = pltpu.roll(gh, n - 1, axis=0)
                    g_prv = pltpu.roll(gh, 1, axis=0)
                    if rev:
                        e = jnp.where(pos == 0, gh + g_nxt,
                                      jnp.where(pos == 1, gh, jnp.where(pos == 2, 0.0, g_prv)))
                    else:
                        e = jnp.where(pos == 0, g_nxt,
                                      jnp.where(pos == 1, 0.0, jnp.where(pos == 2, gh, gh + g_prv)))
                else:
                    e = -jnp.abs(r.b[hd] - boundary(r.b, hd, m))
                xl = (jnp.where(is_q, qh, kh) * jnp.exp2(e)).astype(BF16)
                acc = acc + mask_ref[1 + li] * _dot_nt(xl, xl)
            r.p[hd] = acc

    for ci, m in enumerate(COARSE_LEVELS):
        if ci < len(side_work):
            side_work[ci]()
        is_q_blk = lambda jb: (jb % 2 == 1) != rev
        groups = range(n // (2 * m))
        q_rows = [slice(c * 2 * m + (0 if rev else m), c * 2 * m + (m if rev else 2 * m)) for c in groups]
        k_cols = [slice(c * 2 * m + (m if rev else 0), c * 2 * m + (2 * m if rev else m)) for c in groups]
        for hd in heads:
            for r in seq_refs:
                qh = r.q[:, _lanes(hd)]
                kh = r.k[hd]
                sel = jnp.concatenate(
                    [(qh if is_q_blk(jb) else kh)[jb * m:(jb + 1) * m] for jb in range(n // m)], axis=0)
                xl = (sel * jnp.exp2(-jnp.abs(r.b[hd] - boundary(r.b, hd, m)))).astype(BF16)
                if m >= BF16_ROWS:
                    lhs = jnp.concatenate([xl[rs] for rs in q_rows], axis=0) if len(q_rows) > 1 \
                        else xl[q_rows[0]]
                    s = _dot_nt(lhs, xl)
                    s_rows = [slice(c * m, (c + 1) * m) for c in groups]
                else:
                    s = _dot_nt(xl, xl)
                    s_rows = q_rows
                for rs, sr, kc in zip(q_rows, s_rows, k_cols):
                    r.p[hd, rs, kc] = s[sr, kc]

    for hd in heads:
        for r in seq_refs:
            qh = r.q[:, _lanes(hd)]
            b = r.b[hd]
            b_edge = r.b[hd, pl.ds(edge, 1), :]
            qe = (qh * jnp.exp2(b)).astype(BF16)
            ke = (r.k[hd] * jnp.exp2(b_edge - b)).astype(BF16)
            st = r.st[_lanes(hd), :]
            vh = r.v[:, _lanes(hd)]
            r.o[:, _lanes(hd)] = _dot(r.p[hd].astype(BF16), vh) + _dot_nt(qe, st.astype(BF16))
            r.st[_lanes(hd), :] = st * jnp.exp2(b_edge) + _dot_tn(vh, ke)


def _hgrn2_scratch(seqs):
    blk = (seqs, N_HEADS, TILE, D_HEAD)
    return [
        pltpu.VMEM(blk, F32),
        pltpu.VMEM(blk, F32),
        pltpu.VMEM(blk, F32),
        pltpu.VMEM((seqs, N_HEADS, TILE, TILE), F32),
    ]


def _const_spec(shape):
    zeros = (0,) * len(shape)
    return pl.BlockSpec(shape, lambda b, j: zeros, pipeline_mode=pl.Buffered(1))


def _fwd_kernel(x_ref, xn_ref, meta_ref, gmix_ref, w_ref, convw_ref, convb_ref, rgw_ref, rgb_ref, lam_ref,
                lbl_ref, mask_ref,
                xc_ref, q_ref, v_ref, hf_ref, of_ref,
                st_scr, hc_scr, xa_scr, a_scr, u_scr, z_scr, k_scr, g_scr, b_scr, p_scr, *, seqs):
    j = pl.program_id(1)
    n_real = pl.num_programs(1) - 1
    is_meta = j == 0
    seq_ids = range(seqs)

    @pl.when(is_meta)
    def _():
        st_scr[...] = jnp.zeros_like(st_scr)
        hc_scr[...] = jnp.zeros_like(hc_scr)
        xa_scr[:, :, 0:HALO, :] = jnp.zeros((seqs, N_HEADS, HALO, D_HEAD), F32)

    mains = [jnp.where(is_meta, meta_ref[...], x_ref[s]) for s in seq_ids]
    halos = [jnp.where(j == n_real, 0.0, xn_ref[s]) for s in seq_ids]
    hx = _rmsnorm(jnp.concatenate(mains + halos, axis=0), gmix_ref[...]).astype(BF16)
    n_main = seqs * TILE
    hx_main = hx[0:n_main, :]

    for c in range(N_HEADS):
        ls = _lanes(c)
        if c % 2 == 0:
            xa = _dot(hx, w_ref[:, c * D_HEAD:(c + 2) * D_HEAD])
        cols = slice((c % 2) * D_HEAD, (c % 2 + 1) * D_HEAD)
        for s in seq_ids:
            xa_scr[s, c, HALO:HALO + TILE, :] = xa[s * TILE:(s + 1) * TILE, cols]
            xa_scr[s, c, HALO + TILE:, :] = xa[n_main + s * HALO:n_main + (s + 1) * HALO, cols]
            acc = convb_ref[:, ls] + convw_ref[0:1, ls] * xa_scr[s, c, pl.ds(HALO - 2, TILE), :]
            for tap in range(1, 4):
                acc = acc + convw_ref[tap:tap + 1, ls] * xa_scr[s, c, pl.ds(HALO - 2 + tap, TILE), :]
            a_scr[s, c] = acc
            xa_scr[s, c, 0:HALO, :] = xa_scr[s, c, TILE:TILE + HALO, :]

    coef = _rglru_coef(lam_ref)
    lb = _lower_bound(lbl_ref[...])
    row = lax.broadcasted_iota(jnp.int32, (TILE, D_HEAD), 0)
    pad = jnp.logical_and(is_meta, row < TILE - N_META)

    for c in range(N_HEADS):
        for s in seq_ids:
            xc = a_scr[s, c]
            xc_ref[s, :, _lanes(c)] = xc
            a, u = _rglru_block(xc, c, rgw_ref, rgb_ref, coef)
            a_scr[s, c] = a
            u_scr[s, c] = jnp.where(pad, 0.0, u)
        if c % 2 == 1:
            i = c // 2
            for part in range(3):
                lo = part * D_MODEL + i * Z_CHUNK
                z_scr[:, lo:lo + Z_CHUNK] = _dot(hx_main, w_ref[:, D_MODEL + lo:D_MODEL + lo + Z_CHUNK])
    for s in seq_ids:
        hc_scr[s, 0:1, :] = _scan_linear(a_scr.at[s], u_scr.at[s], hc_scr[s, 0:1, :], rev=False)
    for c in range(N_HEADS):
        for s in seq_ids:
            hf_ref[s, :, _lanes(c)] = u_scr[s, c]

    for c in range(N_HEADS):
        for s in seq_ids:
            rows = slice(s * TILE, (s + 1) * TILE)
            q_ref[s, :, _lanes(c)] = _silu(z_scr[rows, _lanes(c)])
            _forget_gate_block(z_scr[rows, D_MODEL + c * D_HEAD:D_MODEL + (c + 1) * D_HEAD], lb, c,
                               k_scr.at[s], g_scr.at[s], b_scr.at[s])
            v_ref[s, :, _lanes(c)] = z_scr[rows, 2 * D_MODEL + c * D_HEAD:2 * D_MODEL + (c + 1) * D_HEAD].astype(BF16)

    _hgrn2_tile([_SeqRefs(q_ref.at[s], k_scr.at[s], g_scr.at[s], b_scr.at[s], v_ref.at[s], p_scr.at[s],
                          st_scr.at[s], of_ref.at[s]) for s in seq_ids], mask_ref, rev=False)


def _forward_sweep(x, meta_tile, p, seqs):
    bsz, t, _ = x.shape
    assert bsz % seqs == 0
    nt = t // TILE
    halo_blocks = t // HALO
    tok = pl.BlockSpec((seqs, TILE, D_MODEL), lambda b, j: (b, jnp.maximum(j - 1, 0), 0))
    blk = (seqs, N_HEADS, TILE, D_HEAD)
    out_f32 = jax.ShapeDtypeStruct((bsz, t, D_MODEL), F32)
    return pl.pallas_call(
        functools.partial(_fwd_kernel, seqs=seqs),
        grid=(bsz // seqs, nt + 1),
        in_specs=[
            tok,
            pl.BlockSpec((seqs, HALO, D_MODEL),
                         lambda b, j: (b, jnp.minimum(j * (TILE // HALO), halo_blocks - 1), 0)),
            _const_spec((TILE, D_MODEL)),
            _const_spec((1, D_MODEL)),
            _const_spec((D_MODEL, 4 * D_MODEL)),
            _const_spec((4, D_MODEL)),
            _const_spec((1, D_MODEL)),
            _const_spec((N_HEADS, D_HEAD, 2 * D_HEAD)),
            _const_spec((2, D_MODEL)),
            _const_spec((1, D_MODEL)),
            _const_spec((2, D_MODEL)),
            _const_spec((1 + len(FINE_LEVELS), TILE, TILE)),
        ],
        out_specs=[tok, tok, tok, tok, tok],
        out_shape=[out_f32, out_f32, jax.ShapeDtypeStruct((bsz, t, D_MODEL), BF16), out_f32, out_f32],
        scratch_shapes=[
            pltpu.VMEM((seqs, D_MODEL, D_HEAD), F32),
            pltpu.VMEM((seqs, SUBLANES, D_MODEL), F32),
            pltpu.VMEM((seqs, N_HEADS, TILE + 2 * HALO, D_HEAD), F32),
            pltpu.VMEM(blk, F32),
            pltpu.VMEM(blk, F32),
            pltpu.VMEM((seqs * TILE, 3 * D_MODEL), F32),
        ] + _hgrn2_scratch(seqs),
        compiler_params=pltpu.CompilerParams(
            dimension_semantics=("arbitrary", "arbitrary"), vmem_limit_bytes=VMEM_LIMIT),
        name="fwd_sweep",
    )(x, x, meta_tile, p["gmix"], p["w_fwd"], p["conv_w"], p["conv_b"], p["rgw_f"], p["rgb_f"], p["lam_f"],
      p["lbl_f"], p["mask_f"])


def _bwd_kernel(x_ref, xc_ref, q_ref, v_ref, hf_ref, of_ref, gmix_ref, w_ref, rgw_ref, rgb_ref, lam_ref,
                lbl_ref, mask_ref, hgg_ref, wa_ref, wb_ref, wo_ref,
                y_ref,
                st_scr, hc_scr, z_scr, a_scr, u_scr, ob_scr, ma_scr, ba_scr, bb_scr,
                k_scr, g_scr, b_scr, p_scr, *, seqs):
    j = pl.program_id(1)
    seq_ids = range(seqs)

    @pl.when(j == 0)
    def _():
        st_scr[...] = jnp.zeros_like(st_scr)
        hc_scr[...] = jnp.zeros_like(hc_scr)

    x = jnp.concatenate([x_ref[s] for s in seq_ids], axis=0) if seqs > 1 else x_ref[0]
    hx = _rmsnorm(x, gmix_ref[...]).astype(BF16)

    coef = _rglru_coef(lam_ref)
    for c in range(N_HEADS):
        for s in seq_ids:
            a, u = _rglru_block(xc_ref[s, :, _lanes(c)], c, rgw_ref, rgb_ref, coef)
            a_scr[s, c] = a
            u_scr[s, c] = u
        if c % 2 == 1:
            i = c // 2
            for part in range(5):
                lo = part * D_MODEL + i * Z_CHUNK
                z_scr[:, lo:lo + Z_CHUNK] = _dot(hx, w_ref[:, lo:lo + Z_CHUNK])
    for s in seq_ids:
        hc_scr[s, 0:1, :] = _scan_linear(a_scr.at[s], u_scr.at[s], hc_scr[s, 0:1, :], rev=True)

    lb = _lower_bound(lbl_ref[...])
    for c in range(N_HEADS):
        ls = _lanes(c)
        for s in seq_ids:
            rows = slice(s * TILE, (s + 1) * TILE)
            _forget_gate_block(z_scr[rows, D_MODEL + c * D_HEAD:D_MODEL + (c + 1) * D_HEAD], lb, c,
                               k_scr.at[s], g_scr.at[s], b_scr.at[s])
            ba_scr[rows, ls] = ((hf_ref[s, :, ls] + u_scr[s, c]) * _gelu_tanh(z_scr[rows, ls])).astype(BF16)

    def branch_a_slice(i):
        def run():
            cols = slice(i * Z_CHUNK, (i + 1) * Z_CHUNK)
            gate = _sigmoid(z_scr[:, 3 * D_MODEL + i * Z_CHUNK:3 * D_MODEL + (i + 1) * Z_CHUNK])
            ma_scr[:, cols] = gate * _dot(ba_scr[...], wa_ref[:, cols])
        return run

    side_work = [branch_a_slice(i) for i in range(D_MODEL // Z_CHUNK)]
    _hgrn2_tile([_SeqRefs(q_ref.at[s], k_scr.at[s], g_scr.at[s], b_scr.at[s], v_ref.at[s], p_scr.at[s],
                          st_scr.at[s], ob_scr.at[pl.ds(s * TILE, TILE), :]) for s in seq_ids],
                mask_ref, rev=True, side_work=side_work)

    for c in range(N_HEADS):
        ls = _lanes(c)
        for s in seq_ids:
            rows = slice(s * TILE, (s + 1) * TILE)
            o = of_ref[s, :, ls] + ob_scr[rows, ls]
            o = o * lax.rsqrt(jnp.mean(o * o, axis=-1, keepdims=True) + EPS)
            og = z_scr[rows, 2 * D_MODEL + c * D_HEAD:2 * D_MODEL + (c + 1) * D_HEAD]
            bb_scr[rows, ls] = (o * hgg_ref[:, ls] * _silu(og)).astype(BF16)

    merged = ma_scr[...] + _sigmoid(z_scr[:, 4 * D_MODEL:5 * D_MODEL]) * _dot(bb_scr[...], wb_ref[...])
    y = x + _dot(merged.astype(BF16), wo_ref[...])
    for s in seq_ids:
        y_ref[s] = y[s * TILE:(s + 1) * TILE, :]


def _backward_sweep(x, xc, q, v, hf, of, p, seqs):
    bsz, t, _ = x.shape
    assert bsz % seqs == 0
    nt = t // TILE
    tok = pl.BlockSpec((seqs, TILE, D_MODEL), lambda b, j: (b, nt - 1 - j, 0))
    blk = (seqs, N_HEADS, TILE, D_HEAD)
    rows = seqs * TILE
    return pl.pallas_call(
        functools.partial(_bwd_kernel, seqs=seqs),
        grid=(bsz // seqs, nt),
        in_specs=[
            tok, tok, tok, tok, tok, tok,
            _const_spec((1, D_MODEL)),
            _const_spec((D_MODEL, 5 * D_MODEL)),
            _const_spec((N_HEADS, D_HEAD, 2 * D_HEAD)),
            _const_spec((2, D_MODEL)),
            _const_spec((1, D_MODEL)),
            _const_spec((2, D_MODEL)),
            _const_spec((1 + len(FINE_LEVELS), TILE, TILE)),
            _const_spec((1, D_MODEL)),
            _const_spec((D_MODEL, D_MODEL)),
            _const_spec((D_MODEL, D_MODEL)),
            _const_spec((D_MODEL, D_MODEL)),
        ],
        out_specs=tok,
        out_shape=jax.ShapeDtypeStruct((bsz, t, D_MODEL), F32),
        scratch_shapes=[
            pltpu.VMEM((seqs, D_MODEL, D_HEAD), F32),
            pltpu.VMEM((seqs, SUBLANES, D_MODEL), F32),
            pltpu.VMEM((rows, 5 * D_MODEL), F32),
            pltpu.VMEM(blk, F32),
            pltpu.VMEM(blk, F32),
            pltpu.VMEM((rows, D_MODEL), F32),
            pltpu.VMEM((rows, D_MODEL), F32),
            pltpu.VMEM((rows, D_MODEL), BF16),
            pltpu.VMEM((rows, D_MODEL), BF16),
        ] + _hgrn2_scratch(seqs),
        compiler_params=pltpu.CompilerParams(
            dimension_semantics=("arbitrary", "arbitrary"), vmem_limit_bytes=VMEM_LIMIT),
        name="bwd_sweep",
    )(x, xc, q, v, hf, of, p["gmix"], p["w_bwd"], p["rgw_b"], p["rgb_b"], p["lam_b"], p["lbl_b"],
      p["mask_b"], p["hgg"], p["wa"], p["wb"], p["wo"])


def _mlp_kernel(x_ref, gmlp_ref, w1_ref, w2_ref, gfin_ref, y_ref):
    x = x_ref[...]
    hx = _rmsnorm(x, gmlp_ref[...]).astype(BF16)
    acc = x
    for c in range(D_FF // FF_CHUNK):
        cols = slice(c * FF_CHUNK, (c + 1) * FF_CHUNK)
        hm = jnp.maximum(_dot(hx, w1_ref[:, cols]), 0.0)
        acc = acc + _dot((hm * hm).astype(BF16), w2_ref[cols, :])
    y_ref[...] = _rmsnorm(acc, gfin_ref[...])


def _channel_mixer(x, p):
    bsz, t, _ = x.shape
    rows = bsz * t
    x2 = x.reshape(rows, D_MODEL)
    tok = pl.BlockSpec((MLP_TILE, D_MODEL), lambda i: (i, 0))
    const2 = lambda i: (0, 0)
    vec = pl.BlockSpec((1, D_MODEL), const2)
    y = pl.pallas_call(
        _mlp_kernel,
        grid=(rows // MLP_TILE,),
        in_specs=[tok, vec,
                  pl.BlockSpec((D_MODEL, D_FF), const2, pipeline_mode=pl.Buffered(1)),
                  pl.BlockSpec((D_FF, D_MODEL), const2, pipeline_mode=pl.Buffered(1)), vec],
        out_specs=tok,
        out_shape=jax.ShapeDtypeStruct((rows, D_MODEL), F32),
        compiler_params=pltpu.CompilerParams(
            dimension_semantics=("arbitrary",), vmem_limit_bytes=VMEM_LIMIT),
        name="channel_mixer",
    )(x2, p["gmlp"], p["w_mlp1"], p["w_mlp2"], p["gfin"])
    return y.reshape(bsz, t, D_MODEL)


def _prepare_params(meta_tokens, hg_lb_logits, norm_mix_g, w_in, conv_w, conv_b, rg_wa, rg_ba, rg_wx, rg_bx,
                    rg_lambda, hg_norm_g, w_branch_a, w_branch_b, w_out, norm_mlp_g, w_mlp1, w_mlp2,
                    final_norm_g):
    d = D_MODEL
    w = w_in[0].astype(BF16)
    col = lambda i: w[:, i * d:(i + 1) * d]
    row = lambda a: a.reshape(1, d).astype(F32)
    rgw = lambda k: jnp.concatenate([rg_wa[0, k], rg_wx[0, k]], axis=-1).astype(BF16)
    rgb = lambda k: jnp.stack([rg_ba[0, k], rg_bx[0, k]]).astype(F32)
    meta_tile = jnp.concatenate(
        [jnp.zeros((TILE - N_META, d), F32), meta_tokens.astype(F32)], axis=0)
    p = dict(
        gmix=row(norm_mix_g[0]),
        w_fwd=jnp.concatenate([col(0), col(2), col(3), col(5)], axis=1),
        w_bwd=jnp.concatenate([col(1), col(4), col(6), col(7), col(8)], axis=1),
        conv_w=conv_w[0].astype(F32), conv_b=row(conv_b[0]),
        rgw_f=rgw(0), rgw_b=rgw(1), rgb_f=rgb(0), rgb_b=rgb(1),
        lam_f=row(rg_lambda[0, 0]), lam_b=row(rg_lambda[0, 1]),
        lbl_f=hg_lb_logits[:, 0, :].astype(F32), lbl_b=hg_lb_logits[:, 1, :].astype(F32),
        mask_f=jnp.asarray(_fine_masks(TILE, False)), mask_b=jnp.asarray(_fine_masks(TILE, True)),
        hgg=row(hg_norm_g[0]),
        wa=w_branch_a[0].astype(BF16), wb=w_branch_b[0].astype(BF16), wo=w_out[0].astype(BF16),
        gmlp=row(norm_mlp_g[0]), w_mlp1=w_mlp1[0].astype(BF16), w_mlp2=w_mlp2[0].astype(BF16),
        gfin=row(final_norm_g),
    )
    return meta_tile, p


def _seqs_per_step(bsz, cap):
    return max(n for n in range(1, cap + 1) if bsz % n == 0)


def _encode(x, meta_tile, p):
    bsz, t, _ = x.shape
    assert t % TILE == 0 and (bsz * t) % MLP_TILE == 0
    xc, q, v, hf, of = _forward_sweep(x, meta_tile, p, _seqs_per_step(bsz, FWD_SEQS_PER_STEP))
    x_mid = _backward_sweep(x, xc, q, v, hf, of, p, _seqs_per_step(bsz, BWD_SEQS_PER_STEP))
    return _channel_mixer(x_mid, p)


def kernel(x_prompt, x_sample, meta_tokens, hg_lb_logits, norm_mix_g, w_in, conv_w, conv_b, rg_wa, rg_ba, rg_wx,
           rg_bx, rg_lambda, hg_norm_g, w_branch_a, w_branch_b, w_out, norm_mlp_g, w_mlp1, w_mlp2, final_norm_g):
    meta_tile, p = _prepare_params(meta_tokens, hg_lb_logits, norm_mix_g, w_in, conv_w, conv_b, rg_wa, rg_ba,
                                   rg_wx, rg_bx, rg_lambda, hg_norm_g, w_branch_a, w_branch_b, w_out,
                                   norm_mlp_g, w_mlp1, w_mlp2, final_norm_g)
    return (_encode(x_prompt, meta_tile, p), _encode(x_sample, meta_tile, p))
```
